```python
import jax, jax.numpy as jnp
from jax import lax
import numpy as np

D_MODEL = 2048
BATCH = 4
SEQ = 4096
DEPTH = 4

N_MIXERS = 3
CHUNK = 128
A_WIDTH = D_MODEL
A_HEADS = 8
A_HEAD_DIM = A_WIDTH // A_HEADS
B_WIDTH = D_MODEL
B_GROUPS = 4
B_GROUP_DIM = B_WIDTH // B_GROUPS
POOL_WINDOWS = (2, 4, 8, 16)
C_WIDTH = D_MODEL
CONV_WIDTH = 31
N_EXPERTS = 16
N_GROUPS = 4
EXPERTS_PER_GROUP = N_EXPERTS // N_GROUPS
TOP_K = 2
D_EXPERT = D_MODEL // 2
EXPERT_BLOCK = 128
RMS_EPS = 1e-6
LN_EPS = 1e-5

kernel_name = 'hybrid_gmlp_pool_conformer_grouped_moe'


def rmsnorm(x, g):
    xf = x.astype(jnp.float32)
    y = xf * lax.rsqrt(jnp.mean(xf * xf, axis=-1, keepdims=True) + RMS_EPS)
    return (y * g.astype(jnp.float32)).astype(x.dtype)


def mixer_gmlp(h, w_in, v_gain, w_s, b_s, w_out):
    bsz, s, _ = h.shape
    z = jax.nn.gelu(h @ w_in)
    u, v = jnp.split(z, 2, axis=-1)
    v = rmsnorm(v, v_gain)
    v5 = v.reshape(bsz, s // CHUNK, CHUNK, A_HEADS, A_HEAD_DIM)
    causal = jnp.tril(jnp.ones((CHUNK, CHUNK), dtype=w_s.dtype))
    sv = jnp.einsum('hts,bcshd->bcthd', w_s * causal, v5) + b_s.T[None, None, :, :, None]
    return (u * sv.reshape(bsz, s, A_WIDTH)) @ w_out


def mixer_pool(h, w_in, w_grp, b_grp, scale, w_out):
    bsz, s, _ = h.shape
    p = h @ w_in
    c = jnp.cumsum(p.astype(jnp.float32), axis=1).reshape(bsz, s, B_GROUPS, B_GROUP_DIM)
    prev = jnp.stack(
        [jnp.pad(c[:, : s - w, g], ((0, 0), (w, 0), (0, 0))) for g, w in enumerate(POOL_WINDOWS)],
        axis=2)
    pos = jnp.arange(s, dtype=jnp.int32)
    win = jnp.array(POOL_WINDOWS, dtype=jnp.int32)
    count = jnp.minimum(pos[:, None] + 1, win[None, :]).astype(jnp.float32)
    pooled = (c - prev) / count[None, :, :, None]
    mix = pooled.astype(p.dtype) - p.reshape(bsz, s, B_GROUPS, B_GROUP_DIM)
    z = jnp.einsum('bsgc,gcd->bsgd', mix, w_grp) + b_grp[None, None]
    z = z.reshape(bsz, s, B_WIDTH) * scale
    return z @ w_out


def mixer_conv(h, w_in, dw_w, dw_b, ln_g, ln_b, w_out):
    z = h @ w_in
    a, gate = jnp.split(z, 2, axis=-1)
    g = a * jax.nn.sigmoid(gate)
    y = lax.conv_general_dilated(
        g, dw_w[:, None, :].astype(g.dtype), window_strides=(1,),
        padding=[(CONV_WIDTH - 1, 0)], dimension_numbers=('NWC', 'WIO', 'NWC'),
        feature_group_count=C_WIDTH) + dw_b
    yf = y.astype(jnp.float32)
    mu = jnp.mean(yf, axis=-1, keepdims=True)
    var = jnp.mean(jnp.square(yf - mu), axis=-1, keepdims=True)
    yn = ((yf - mu) * lax.rsqrt(var + LN_EPS) * ln_g.astype(jnp.float32)
          + ln_b.astype(jnp.float32)).astype(g.dtype)
    return jax.nn.silu(yn) @ w_out


def grouped_moe(y, router_w, router_b, w_gate, w_up, w_down):
    bsz, s, d = y.shape
    n = bsz * s
    xf = y.reshape(n, d)
    probs = jax.nn.softmax((xf @ router_w).astype(jnp.float32), axis=-1)
    sel = (probs + router_b.astype(jnp.float32)).reshape(n, N_GROUPS, EXPERTS_PER_GROUP)
    group_score = jnp.sum(lax.top_k(sel, TOP_K)[0], axis=-1)
    grp = jnp.argmax(group_score, axis=-1).astype(jnp.int32)
    in_group = jnp.take_along_axis(sel, grp[:, None, None], axis=1)[:, 0]
    _, loc = lax.top_k(in_group, TOP_K)
    idx = grp[:, None] * EXPERTS_PER_GROUP + loc.astype(jnp.int32)
    gw = jnp.take_along_axis(probs, idx, axis=-1)
    gw = gw / jnp.sum(gw, axis=-1, keepdims=True)

    n_assign = n * TOP_K
    flat_e = idx.reshape(n_assign)
    flat_tok = jnp.repeat(jnp.arange(n, dtype=jnp.int32), TOP_K)
    flat_w = gw.reshape(n_assign)
    order = jnp.argsort(flat_e)
    se, stok, sw = flat_e[order], flat_tok[order], flat_w[order]
    counts = jnp.bincount(flat_e, length=N_EXPERTS).astype(jnp.int32)
    pcounts = (counts + EXPERT_BLOCK - 1) // EXPERT_BLOCK * EXPERT_BLOCK
    starts = jnp.cumsum(counts) - counts
    pends = jnp.cumsum(pcounts)
    pstarts = pends - pcounts
    dest = pstarts[se] + jnp.arange(n_assign, dtype=jnp.int32) - starts[se]
    n_rows = n_assign + N_EXPERTS * EXPERT_BLOCK
    n_blocks = n_rows // EXPERT_BLOCK
    tok_buf = jnp.full((n_rows,), n, dtype=jnp.int32).at[dest].set(stok)
    w_buf = jnp.zeros((n_rows,), dtype=jnp.float32).at[dest].set(sw)
    block_e = jnp.minimum(
        jnp.searchsorted(pends, jnp.arange(n_blocks, dtype=jnp.int32) * EXPERT_BLOCK, side='right'),
        N_EXPERTS - 1).astype(jnp.int32)
    x_pad = jnp.concatenate([xf, jnp.zeros((1, d), xf.dtype)], axis=0)
    xb = x_pad[tok_buf].reshape(n_blocks, EXPERT_BLOCK, d)

    def expert_block(args):
        xblk, e = args
        hid = jax.nn.silu(xblk @ w_gate[e]) * (xblk @ w_up[e])
        return hid @ w_down[e]

    yb = lax.map(expert_block, (xb, block_e)).reshape(n_rows, d)
    out = jax.ops.segment_sum(yb * w_buf[:, None].astype(yb.dtype), tok_buf, num_segments=n + 1)[:n]
    return out.reshape(bsz, s, d)


def setup_inputs(seed: int = 0) -> dict:
    key = jax.random.key(seed)
    ks = iter(jax.random.split(key, 32))
    nrm = lambda shape, scale: jax.random.normal(next(ks), shape, jnp.float32) * scale
    na, nb, nc = (len(range(m, DEPTH, N_MIXERS)) for m in range(N_MIXERS))
    out_scale = (2.0 * DEPTH) ** -0.5
    return {
        'x': nrm((BATCH, SEQ, D_MODEL), 1.0),
        'norm_mix_g': 1.0 + nrm((DEPTH, D_MODEL), 0.05),
        'norm_ffn_g': 1.0 + nrm((DEPTH, D_MODEL), 0.05),
        'final_g': 1.0 + nrm((D_MODEL,), 0.05),
        'a_w_in': nrm((na, D_MODEL, 2 * A_WIDTH), D_MODEL ** -0.5),
        'a_v_gain': 1.0 + nrm((na, A_WIDTH), 0.05),
        'a_w_s': nrm((na, A_HEADS, CHUNK, CHUNK), 0.5 * CHUNK ** -0.5),
        'a_b_s': 1.0 + nrm((na, A_HEADS, CHUNK), 0.1),
        'a_w_out': nrm((na, A_WIDTH, D_MODEL), A_WIDTH ** -0.5 * out_scale),
        'b_w_in': nrm((nb, D_MODEL, B_WIDTH), D_MODEL ** -0.5),
        'b_w_grp': nrm((nb, B_GROUPS, B_GROUP_DIM, B_GROUP_DIM), B_GROUP_DIM ** -0.5),
        'b_b_grp': nrm((nb, B_GROUPS, B_GROUP_DIM), 0.02),
        'b_scale': 1.0 + nrm((nb, B_WIDTH), 0.1),
        'b_w_out': nrm((nb, B_WIDTH, D_MODEL), B_WIDTH ** -0.5 * out_scale),
        'c_w_in': nrm((nc, D_MODEL, 2 * C_WIDTH), D_MODEL ** -0.5),
        'c_dw_w': nrm((nc, CONV_WIDTH, C_WIDTH), CONV_WIDTH ** -0.5),
        'c_dw_b': nrm((nc, C_WIDTH), 0.02),
        'c_ln_g': 1.0 + nrm((nc, C_WIDTH), 0.05),
        'c_ln_b': nrm((nc, C_WIDTH), 0.02),
        'c_w_out': nrm((nc, C_WIDTH, D_MODEL), C_WIDTH ** -0.5 * out_scale),
        'router_w': nrm((D_MODEL, N_EXPERTS), D_MODEL ** -0.5),
        'router_b': nrm((N_EXPERTS,), 0.01),
        'moe_w_gate': nrm((DEPTH, N_EXPERTS, D_MODEL, D_EXPERT), D_MODEL ** -0.5),
        'moe_w_up': nrm((DEPTH, N_EXPERTS, D_MODEL, D_EXPERT), D_MODEL ** -0.5),
        'moe_w_down': nrm((DEPTH, N_EXPERTS, D_EXPERT, D_MODEL), D_EXPERT ** -0.5 * out_scale),
    }


def reference(x, norm_mix_g, norm_ffn_g, final_g,
              a_w_in, a_v_gain, a_w_s, a_b_s, a_w_out,
              b_w_in, b_w_grp, b_b_grp, b_scale, b_w_out,
              c_w_in, c_dw_w, c_dw_b, c_ln_g, c_ln_b, c_w_out,
              router_w, router_b, moe_w_gate, moe_w_up, moe_w_down):
    h = x
    for i in range(DEPTH):
        m, j = i % N_MIXERS, i // N_MIXERS
        y = rmsnorm(h, norm_mix_g[i])
        if m == 0:
            mix = mixer_gmlp(y, a_w_in[j], a_v_gain[j], a_w_s[j], a_b_s[j], a_w_out[j])
        elif m == 1:
            mix = mixer_pool(y, b_w_in[j], b_w_grp[j], b_b_grp[j], b_scale[j], b_w_out[j])
        else:
            mix = mixer_conv(y, c_w_in[j], c_dw_w[j], c_dw_b[j], c_ln_g[j], c_ln_b[j], c_w_out[j])
        h = h + mix
        y = rmsnorm(h, norm_ffn_g[i])
        h = h + grouped_moe(y, router_w, router_b, moe_w_gate[i], moe_w_up[i], moe_w_down[i])
    return rmsnorm(h, final_g)
```

```python
import functools

import jax
import jax.numpy as jnp
from jax import lax
from jax.experimental import pallas as pl
from jax.experimental.pallas import tpu as pltpu

F32 = jnp.float32
BF16 = jnp.bfloat16

N_MIXERS = 3
CHUNK = 128
A_HEADS = 8
B_GROUPS = 4
POOL_WINDOWS = (2, 4, 8, 16)
CONV_WIDTH = 31
N_EXPERTS = 16
N_GROUPS = 4
EXPERTS_PER_GROUP = N_EXPERTS // N_GROUPS
TOP_K = 2
RMS_EPS = 1e-6
LN_EPS = 1e-5

POOL_HALO = 16
CONV_HALO = 32

TOKEN_TILE = 256
EXPERT_TILE = 256
ROUTE_ROWS = 8
VMEM_LIMIT = 56 * 1024 * 1024


def _resident(shape):
    nd = len(shape)
    return pl.BlockSpec(shape, lambda *_: (0,) * nd, pipeline_mode=pl.Buffered(1))


def _rms(x, g):
    ms = jnp.mean(x * x, axis=-1, keepdims=True)
    return x * lax.rsqrt(ms + RMS_EPS) * g


def _mixer_gmlp(xn, w_in_ref, vg_ref, ws_ref, bs_ref, w_out_ref):
    t, d = xn.shape
    hd = d // A_HEADS
    v = jax.nn.gelu(jnp.dot(xn, w_in_ref[:, d:], preferred_element_type=F32))
    vb = _rms(v, vg_ref[...]).astype(BF16)
    row = lax.broadcasted_iota(jnp.int32, (CHUNK, CHUNK), 0)
    col = lax.broadcasted_iota(jnp.int32, (CHUNK, CHUNK), 1)
    causal = col <= row
    ws = [jnp.where(causal, ws_ref[h], 0.0).astype(BF16) for h in range(A_HEADS)]
    rows = []
    for c in range(t // CHUNK):
        cols = []
        for h in range(A_HEADS):
            blk = vb[c * CHUNK:(c + 1) * CHUNK, h * hd:(h + 1) * hd]
            cols.append(jnp.dot(ws[h], blk, preferred_element_type=F32) + bs_ref[h])
        rows.append(jnp.concatenate(cols, axis=1))
    sv = jnp.concatenate(rows, axis=0)
    u = jax.nn.gelu(jnp.dot(xn, w_in_ref[:, :d], preferred_element_type=F32))
    return jnp.dot((u * sv).astype(BF16), w_out_ref[...], preferred_element_type=F32)


def _mixer_pool(xn, pos0, w_in_ref, w_grp_ref, b_grp_ref, scale_ref, w_out_ref, halo_ref):
    t, d = xn.shape
    gd = d // B_GROUPS
    p = jnp.dot(xn, w_in_ref[...], preferred_element_type=F32)

    @pl.when(pos0 == 0)
    def _():
        halo_ref[...] = jnp.zeros_like(halo_ref)

    ext = jnp.concatenate([halo_ref[...], p], axis=0)
    halo_ref[...] = p[t - POOL_HALO:, :]
    pos = pos0 + lax.broadcasted_iota(jnp.int32, (t, 1), 0)
    zs = []
    for g, w in enumerate(POOL_WINDOWS):
        e = ext[:, g * gd:(g + 1) * gd]
        acc, s = e, 1
        while s < w:
            acc = acc[s:, :] + acc[:-s, :]
            s *= 2
        win = acc[POOL_HALO - w + 1:POOL_HALO - w + 1 + t, :]
        count = jnp.minimum(pos + 1, w).astype(F32)
        mix = win / count - p[:, g * gd:(g + 1) * gd]
        z = jnp.dot(mix.astype(BF16), w_grp_ref[g], preferred_element_type=F32) + b_grp_ref[g]
        zs.append(z)
    z = jnp.concatenate(zs, axis=1) * scale_ref[...]
    return jnp.dot(z.astype(BF16), w_out_ref[...], preferred_element_type=F32)


def _mixer_conv(xn, pos0, w_in_ref, dw_w_ref, dw_b_ref, ln_g_ref, ln_b_ref, w_out_ref, halo_ref):
    t, d = xn.shape
    a = jnp.dot(xn, w_in_ref[:, :d], preferred_element_type=F32)
    gate = jnp.dot(xn, w_in_ref[:, d:], preferred_element_type=F32)
    glu = a * jax.nn.sigmoid(gate)

    @pl.when(pos0 == 0)
    def _():
        halo_ref[...] = jnp.zeros_like(halo_ref)

    ext = jnp.concatenate([halo_ref[...], glu], axis=0)
    halo_ref[...] = glu[t - CONV_HALO:, :]
    off = CONV_HALO - (CONV_WIDTH - 1)
    y = jnp.zeros((t, d), F32) + dw_b_ref[...]
    for k in range(CONV_WIDTH):
        y = y + ext[off + k:off + k + t, :] * dw_w_ref[k:k + 1, :]
    mu = jnp.mean(y, axis=-1, keepdims=True)
    yc = y - mu
    var = jnp.mean(yc * yc, axis=-1, keepdims=True)
    yn = yc * lax.rsqrt(var + LN_EPS) * ln_g_ref[...] + ln_b_ref[...]
    return jnp.dot(jax.nn.silu(yn).astype(BF16), w_out_ref[...], preferred_element_type=F32)


def _route(y2b, rwt_ref, rb_ref, carry_ref):
    t = y2b.shape[0]
    logits = lax.dot_general(rwt_ref[...], y2b, (((1,), (1,)), ((), ())), preferred_element_type=F32)
    m = jnp.max(logits, axis=0, keepdims=True)
    ex = jnp.exp(logits - m)
    probs = ex / jnp.sum(ex, axis=0, keepdims=True)
    sel = probs + rb_ref[...]
    srow = [sel[e:e + 1, :] for e in range(N_EXPERTS)]
    prow = [probs[e:e + 1, :] for e in range(N_EXPERTS)]

    def top2_sum(v0, v1, v2, v3):
        a, b = jnp.maximum(v0, v1), jnp.minimum(v0, v1)
        c, dd = jnp.maximum(v2, v3), jnp.minimum(v2, v3)
        return jnp.maximum(a, c) + jnp.maximum(jnp.minimum(a, c), jnp.maximum(b, dd))

    gscore = [top2_sum(*srow[4 * g:4 * g + 4]) for g in range(N_GROUPS)]
    best, grp = gscore[0], jnp.zeros((1, t), jnp.int32)
    for g in range(1, N_GROUPS):
        better = gscore[g] > best
        best = jnp.where(better, gscore[g], best)
        grp = jnp.where(better, g, grp)
    sg, pg = [], []
    for j in range(EXPERTS_PER_GROUP):
        s_j, p_j = srow[j], prow[j]
        for g in range(1, N_GROUPS):
            s_j = jnp.where(grp == g, srow[4 * g + j], s_j)
            p_j = jnp.where(grp == g, prow[4 * g + j], p_j)
        sg.append(s_j)
        pg.append(p_j)
    b0, i0 = sg[0], jnp.zeros((1, t), jnp.int32)
    for j in range(1, EXPERTS_PER_GROUP):
        better = sg[j] > b0
        b0 = jnp.where(better, sg[j], b0)
        i0 = jnp.where(better, j, i0)
    b1, i1 = jnp.full((1, t), -jnp.inf, F32), jnp.zeros((1, t), jnp.int32)
    for j in range(EXPERTS_PER_GROUP):
        better = (i0 != j) & (sg[j] > b1)
        b1 = jnp.where(better, sg[j], b1)
        i1 = jnp.where(better, j, i1)
    p0, p1 = jnp.zeros((1, t), F32), jnp.zeros((1, t), F32)
    for j in range(EXPERTS_PER_GROUP):
        p0 = jnp.where(i0 == j, pg[j], p0)
        p1 = jnp.where(i1 == j, pg[j], p1)
    e0 = grp * EXPERTS_PER_GROUP + i0
    e1 = grp * EXPERTS_PER_GROUP + i1
    den = p0 + p1
    g0, g1 = p0 / den, p1 / den

    eid = lax.broadcasted_iota(jnp.int32, (N_EXPERTS, t), 0)
    hit0, hit1 = eid == e0, eid == e1
    onehot = (hit0 | hit1).astype(BF16)
    src = lax.broadcasted_iota(jnp.int32, (t, t), 0)
    dst = lax.broadcasted_iota(jnp.int32, (t, t), 1)
    before = (src < dst).astype(BF16)
    excl = jnp.dot(onehot, before, preferred_element_type=F32) + carry_ref[...]
    r0 = jnp.sum(jnp.where(hit0, excl, 0.0), axis=0, keepdims=True)
    r1 = jnp.sum(jnp.where(hit1, excl, 0.0), axis=0, keepdims=True)
    carry_ref[...] += jnp.sum(onehot.astype(F32), axis=1, keepdims=True)
    zero = jnp.zeros((1, t), F32)
    return jnp.concatenate([e0.astype(F32), e1.astype(F32), r0, r1, g0, g1, zero, zero], axis=0)


def _mixer_kernel(kind, tiles_per_seq, *refs):
    h_ref, gmix_ref, gffn_ref, rwt_ref, rb_ref = refs[:5]
    n_w = {"gmlp": 5, "pool": 5, "conv": 6}[kind]
    w_refs = refs[5:5 + n_w]
    hmid_ref, y2_ref, route_ref, counts_ref = refs[5 + n_w:9 + n_w]
    scratch = refs[9 + n_w:]
    carry_ref = scratch[0]
    i = pl.program_id(0)
    t = h_ref.shape[0]
    pos0 = (i % tiles_per_seq) * t

    @pl.when(i == 0)
    def _():
        carry_ref[...] = jnp.zeros_like(carry_ref)

    h = h_ref[...]
    xn = _rms(h, gmix_ref[...]).astype(BF16)
    if kind == "gmlp":
        mix = _mixer_gmlp(xn, *w_refs)
    elif kind == "pool":
        mix = _mixer_pool(xn, pos0, *w_refs, scratch[1])
    else:
        mix = _mixer_conv(xn, pos0, *w_refs, scratch[1])
    hmid = h + mix
    hmid_ref[...] = hmid
    y2 = _rms(hmid, gffn_ref[...])
    y2_ref[...] = y2
    route_ref[...] = _route(y2.astype(BF16), rwt_ref, rb_ref, carry_ref)
    counts_ref[...] = jnp.broadcast_to(carry_ref[...], counts_ref.shape)


def _mixer_call(kind, h, gmix, gffn, rwt, rb, weights, seq_len):
    n, d = h.shape
    t = TOKEN_TILE
    tiles_per_seq = seq_len // t
    row = lambda i: (i, 0)
    in_specs = [pl.BlockSpec((t, d), row), _resident((1, d)), _resident((1, d)),
                _resident(rwt.shape), _resident(rb.shape)] + [_resident(w.shape) for w in weights]
    out_specs = [pl.BlockSpec((t, d), row), pl.BlockSpec((t, d), row),
                 pl.BlockSpec((ROUTE_ROWS, t), lambda i: (0, i)),
                 pl.BlockSpec((N_EXPERTS, 128), lambda i: (0, 0))]
    out_shape = [jax.ShapeDtypeStruct((n, d), F32), jax.ShapeDtypeStruct((n, d), F32),
                 jax.ShapeDtypeStruct((ROUTE_ROWS, n), F32), jax.ShapeDtypeStruct((N_EXPERTS, 128), F32)]
    scratch = [pltpu.VMEM((N_EXPERTS, 1), F32)]
    if kind == "pool":
        scratch.append(pltpu.VMEM((POOL_HALO, d), F32))
    elif kind == "conv":
        scratch.append(pltpu.VMEM((CONV_HALO, d), F32))
    return pl.pallas_call(
        functools.partial(_mixer_kernel, kind, tiles_per_seq),
        grid=(n // t,), in_specs=in_specs, out_specs=out_specs, out_shape=out_shape,
        scratch_shapes=scratch, name=f"mixer_{kind}",
        compiler_params=pltpu.CompilerParams(dimension_semantics=("arbitrary",), vmem_limit_bytes=VMEM_LIMIT),
    )(h, gmix, gffn, rwt, rb, *weights)


def _row_copy(src_ref, src_row, dst_ref, dst_row, sem):
    return pltpu.make_async_copy(src_ref.at[pl.ds(src_row, 1), :], dst_ref.at[pl.ds(dst_row, 1), :], sem)


def _scatter_kernel(dest_ref, y_ref, xb_in_ref, xb_ref, sem):
    del xb_in_ref
    t = y_ref.shape[0]
    n = t * pl.num_programs(0)
    base = pl.program_id(0) * t

    def issue(r, c):
        for k in range(TOP_K):
            _row_copy(y_ref, r, xb_ref, dest_ref[k * n + base + r], sem).start()
        return c

    lax.fori_loop(0, t, issue, 0, unroll=8)

    def drain(r, c):
        for k in range(TOP_K):
            _row_copy(y_ref, r, xb_ref, dest_ref[k * n + base + r], sem).wait()
        return c

    lax.fori_loop(0, t, drain, 0, unroll=8)


def _scatter_call(dest, y2, n_rows):
    n, d = y2.shape
    t = TOKEN_TILE
    xb0 = jnp.zeros((n_rows, d), F32)
    return pl.pallas_call(
        _scatter_kernel,
        grid_spec=pltpu.PrefetchScalarGridSpec(
            num_scalar_prefetch=1, grid=(n // t,),
            in_specs=[pl.BlockSpec((t, d), lambda i, dest: (i, 0)), pl.BlockSpec(memory_space=pl.ANY)],
            out_specs=pl.BlockSpec(memory_space=pl.ANY),
            scratch_shapes=[pltpu.SemaphoreType.DMA]),
        out_shape=jax.ShapeDtypeStruct((n_rows, d), F32),
        input_output_aliases={2: 0}, name="moe_scatter",
        compiler_params=pltpu.CompilerParams(dimension_semantics=("arbitrary",)),
    )(dest, y2, xb0)


def _expert_kernel(block_e_ref, n_used_ref, x_ref, wg_ref, wu_ref, wd_ref, o_ref):
    del block_e_ref

    @pl.when(pl.program_id(0) < n_used_ref[0])
    def _():
        x = x_ref[...].astype(BF16)
        gate = jnp.dot(x, wg_ref[0], preferred_element_type=F32)
        up = jnp.dot(x, wu_ref[0], preferred_element_type=F32)
        hid = (jax.nn.silu(gate) * up).astype(BF16)
        o_ref[...] = jnp.dot(hid, wd_ref[0], preferred_element_type=F32)

    @pl.when(pl.program_id(0) >= n_used_ref[0])
    def _():
        o_ref[...] = jnp.zeros_like(o_ref)


def _expert_call(block_e, n_used, xb, wg, wu, wd):
    n_rows, d = xb.shape
    f = wg.shape[-1]
    tm = EXPERT_TILE
    wmap = lambda b, be, nu: (be[b], 0, 0)
    return pl.pallas_call(
        _expert_kernel,
        grid_spec=pltpu.PrefetchScalarGridSpec(
            num_scalar_prefetch=2, grid=(n_rows // tm,),
            in_specs=[pl.BlockSpec((tm, d), lambda b, be, nu: (b, 0)),
                      pl.BlockSpec((1, d, f), wmap), pl.BlockSpec((1, d, f), wmap), pl.BlockSpec((1, f, d), wmap)],
            out_specs=pl.BlockSpec((tm, d), lambda b, be, nu: (b, 0))),
        out_shape=jax.ShapeDtypeStruct((n_rows, d), F32), name="moe_experts",
        compiler_params=pltpu.CompilerParams(dimension_semantics=("arbitrary",), vmem_limit_bytes=VMEM_LIMIT),
    )(block_e, n_used, xb, wg, wu, wd)


def _combine_kernel(final, dest_ref, h_ref, gate_ref, g_ref, yb_ref, o_ref, buf0, buf1, sem):
    t = h_ref.shape[0]
    n = t * pl.num_programs(0)
    base = pl.program_id(0) * t
    bufs = (buf0, buf1)

    def issue(r, c):
        for k in range(TOP_K):
            _row_copy(yb_ref, dest_ref[k * n + base + r], bufs[k], r, sem).start()
        return c

    lax.fori_loop(0, t, issue, 0, unroll=8)

    def drain(r, c):
        for k in range(TOP_K):
            _row_copy(yb_ref, dest_ref[k * n + base + r], bufs[k], r, sem).wait()
        return c

    lax.fori_loop(0, t, drain, 0, unroll=8)
    gates = gate_ref[...]
    out = h_ref[...] + gates[:, 0:1] * buf0[...] + gates[:, 1:2] * buf1[...]
    if final:
        out = _rms(out, g_ref[...])
    o_ref[...] = out


def _combine_call(final, dest, h, gates, g, yb):
    n, d = h.shape
    t = TOKEN_TILE
    row = lambda i, dest: (i, 0)
    return pl.pallas_call(
        functools.partial(_combine_kernel, final),
        grid_spec=pltpu.PrefetchScalarGridSpec(
            num_scalar_prefetch=1, grid=(n // t,),
            in_specs=[pl.BlockSpec((t, d), row), pl.BlockSpec((t, TOP_K), row),
                      pl.BlockSpec((1, d), lambda i, dest: (0, 0)), pl.BlockSpec(memory_space=pl.ANY)],
            out_specs=pl.BlockSpec((t, d), row),
            scratch_shapes=[pltpu.VMEM((t, d), F32), pltpu.VMEM((t, d), F32), pltpu.SemaphoreType.DMA]),
        out_shape=jax.ShapeDtypeStruct((n, d), F32), name="moe_combine",
        compiler_params=pltpu.CompilerParams(dimension_semantics=("arbitrary",)),
    )(dest, h, gates, g, yb)


def _plan(route, counts, n_rows):
    e = route[0:2].astype(jnp.int32)
    rank = route[2:4].astype(jnp.int32)
    gates = route[4:6].T
    cnt = counts[:, 0].astype(jnp.int32)
    pcnt = (cnt + EXPERT_TILE - 1) // EXPERT_TILE * EXPERT_TILE
    pends = jnp.cumsum(pcnt)
    pstarts = pends - pcnt
    dest = (pstarts[e] + rank).reshape(-1)
    blocks = jnp.arange(n_rows // EXPERT_TILE, dtype=jnp.int32)
    n_used = (pends[-1] // EXPERT_TILE).astype(jnp.int32)
    last = jnp.maximum(n_used - 1, 0)
    block_e = jnp.searchsorted(pends, jnp.minimum(blocks, last) * EXPERT_TILE, side="right")
    block_e = jnp.minimum(block_e, N_EXPERTS - 1).astype(jnp.int32)
    return dest, gates, block_e, n_used.reshape(1)


def kernel(x, norm_mix_g, norm_ffn_g, final_g, a_w_in, a_v_gain, a_w_s, a_b_s, a_w_out, b_w_in, b_w_grp, b_b_grp, b_scale, b_w_out, c_w_in, c_dw_w, c_dw_b, c_ln_g, c_ln_b, c_w_out, router_w, router_b, moe_w_gate, moe_w_up, moe_w_down):
    bsz, seq_len, d = x.shape
    depth = norm_mix_g.shape[0]
    n = bsz * seq_len
    n_rows = n * TOP_K + N_EXPERTS * EXPERT_TILE
    h = x.reshape(n, d)
    rwt = router_w.T.astype(BF16)
    rb = router_b.reshape(N_EXPERTS, 1)
    for i in range(depth):
        m, j = i % N_MIXERS, i // N_MIXERS
        if m == 0:
            kind = "gmlp"
            weights = (a_w_in[j].astype(BF16), a_v_gain[j].reshape(1, d), a_w_s[j],
                       a_b_s[j].reshape(A_HEADS, CHUNK, 1), a_w_out[j].astype(BF16))
        elif m == 1:
            kind = "pool"
            weights = (b_w_in[j].astype(BF16), b_w_grp[j].astype(BF16),
                       b_b_grp[j].reshape(B_GROUPS, 1, d // B_GROUPS), b_scale[j].reshape(1, d),
                       b_w_out[j].astype(BF16))
        else:
            kind = "conv"
            weights = (c_w_in[j].astype(BF16), c_dw_w[j], c_dw_b[j].reshape(1, d), c_ln_g[j].reshape(1, d),
                       c_ln_b[j].reshape(1, d), c_w_out[j].astype(BF16))
        hmid, y2, route, counts = _mixer_call(
            kind, h, norm_mix_g[i].reshape(1, d), norm_ffn_g[i].reshape(1, d), rwt, rb, weights, seq_len)
        dest, gates, block_e, n_used = _plan(route, counts, n_rows)
        xb = _scatter_call(dest, y2, n_rows)
        yb = _expert_call(block_e, n_used, xb, moe_w_gate[i].astype(BF16), moe_w_up[i].astype(BF16),
                          moe_w_down[i].astype(BF16))
        h = _combine_call(i == depth - 1, dest, hmid, gates, final_g.reshape(1, d), yb)
    return h.reshape(bsz, seq_len, d)
```

```python
import functools

import jax
import jax.numpy as jnp
from jax import lax
from jax.experimental import pallas as pl
from jax.experimental.pallas import tpu as pltpu

F32 = jnp.float32
BF16 = jnp.bfloat16

N_MIXERS = 3
CHUNK = 128
A_HEADS = 8
B_GROUPS = 4
POOL_WINDOWS = (2, 4, 8, 16)
CONV_WIDTH = 31
N_EXPERTS = 16
N_GROUPS = 4
EXPERTS_PER_GROUP = N_EXPERTS // N_GROUPS
TOP_K = 2
RMS_EPS = 1e-6
LN_EPS = 1e-5

SUBLANES = 8
POOL_HALO = 16
CONV_HALO = 32
CONV_COLS = 512
CONV_ROWS = 32

TOKEN_TILE = 256
EXPERT_TILE = 256
ROUTE_ROWS = 8
VMEM_LIMIT = 56 * 1024 * 1024


def _resident(shape):
    nd = len(shape)
    return pl.BlockSpec(shape, lambda *_: (0,) * nd, pipeline_mode=pl.Buffered(1))


def _rms(x, g):
    ms = jnp.mean(x * x, axis=-1, keepdims=True)
    return x * lax.rsqrt(ms + RMS_EPS) * g


def _mixer_gmlp(xn, w_in_ref, vg_ref, ws_ref, bs_ref, w_out_ref):
    t, d = xn.shape
    hd = d // A_HEADS
    v = jax.nn.gelu(jnp.dot(xn, w_in_ref[:, d:], preferred_element_type=F32))
    vb = _rms(v, vg_ref[...]).astype(BF16)
    row = lax.broadcasted_iota(jnp.int32, (CHUNK, CHUNK), 0)
    col = lax.broadcasted_iota(jnp.int32, (CHUNK, CHUNK), 1)
    causal = col <= row
    ws = [jnp.where(causal, ws_ref[h], 0.0).astype(BF16) for h in range(A_HEADS)]
    rows = []
    for c in range(t // CHUNK):
        cols = []
        for h in range(A_HEADS):
            blk = vb[c * CHUNK:(c + 1) * CHUNK, h * hd:(h + 1) * hd]
            cols.append(jnp.dot(ws[h], blk, preferred_element_type=F32) + bs_ref[h])
        rows.append(jnp.concatenate(cols, axis=1))
    sv = jnp.concatenate(rows, axis=0)
    u = jax.nn.gelu(jnp.dot(xn, w_in_ref[:, :d], preferred_element_type=F32))
    return jnp.dot((u * sv).astype(BF16), w_out_ref[...], preferred_element_type=F32)


def _mixer_pool(xn, pos0, w_in_ref, w_grp_ref, b_grp_ref, scale_ref, w_out_ref, halo_ref):
    t, d = xn.shape
    gd = d // B_GROUPS
    p = jnp.dot(xn, w_in_ref[...], preferred_element_type=F32)

    @pl.when(pos0 == 0)
    def _():
        halo_ref[...] = jnp.zeros_like(halo_ref)

    ext = jnp.concatenate([halo_ref[...], p], axis=0)
    halo_ref[...] = p[t - POOL_HALO:, :]
    pos = pos0 + lax.broadcasted_iota(jnp.int32, (t, 1), 0)
    zs = []
    for g, w in enumerate(POOL_WINDOWS):
        e = ext[:, g * gd:(g + 1) * gd]
        acc, s = e, 1
        while s < w:
            acc = acc[s:, :] + acc[:-s, :]
            s *= 2
        win = acc[POOL_HALO - w + 1:POOL_HALO - w + 1 + t, :]
        count = jnp.minimum(pos + 1, w).astype(F32)
        mix = win / count - p[:, g * gd:(g + 1) * gd]
        z = jnp.dot(mix.astype(BF16), w_grp_ref[g], preferred_element_type=F32) + b_grp_ref[g]
        zs.append(z)
    z = jnp.concatenate(zs, axis=1) * scale_ref[...]
    return jnp.dot(z.astype(BF16), w_out_ref[...], preferred_element_type=F32)


def _mixer_conv(xn, pos0, w_in_ref, dw_w_ref, dw_b_ref, ln_g_ref, ln_b_ref, w_out_ref, ext_ref, sh_ref, y_ref):
    t, d = xn.shape
    a = jnp.dot(xn, w_in_ref[:, :d], preferred_element_type=F32)
    gate = jnp.dot(xn, w_in_ref[:, d:], preferred_element_type=F32)

    @pl.when(pos0 == 0)
    def _():
        ext_ref[0:CONV_HALO, :] = jnp.zeros((CONV_HALO, d), F32)

    ext_ref[CONV_HALO:, :] = a * jax.nn.sigmoid(gate)
    off = CONV_HALO - (CONV_WIDTH - 1)
    span = t + CONV_HALO - SUBLANES
    for cb in range(d // CONV_COLS):
        cols = slice(cb * CONV_COLS, (cb + 1) * CONV_COLS)
        for q in range(1, SUBLANES):
            sh_ref[q - 1, 0:span, :] = ext_ref[q:q + span, cols]

        def row_block(j, c, cols=cols):
            r0 = pl.multiple_of(j * CONV_ROWS, CONV_ROWS)
            acc = jnp.broadcast_to(dw_b_ref[:, cols], (CONV_ROWS, CONV_COLS))
            for k in range(CONV_WIDTH):
                q, base = (off + k) % SUBLANES, r0 + (off + k) // SUBLANES * SUBLANES
                if q == 0:
                    tap = ext_ref[pl.ds(base, CONV_ROWS), cols]
                else:
                    tap = sh_ref[q - 1, pl.ds(base, CONV_ROWS), :]
                acc = acc + tap * dw_w_ref[k:k + 1, cols]
            y_ref[pl.ds(r0, CONV_ROWS), cols] = acc
            return c

        lax.fori_loop(0, t // CONV_ROWS, row_block, 0)
    ext_ref[0:CONV_HALO, :] = ext_ref[t:t + CONV_HALO, :]
    y = y_ref[...]
    mu = jnp.mean(y, axis=-1, keepdims=True)
    yc = y - mu
    var = jnp.mean(yc * yc, axis=-1, keepdims=True)
    yn = yc * lax.rsqrt(var + LN_EPS) * ln_g_ref[...] + ln_b_ref[...]
    return jnp.dot(jax.nn.silu(yn).astype(BF16), w_out_ref[...], preferred_element_type=F32)


def _route(y2b, rwt_ref, rb_ref, carry_ref):
    t = y2b.shape[0]
    logits = lax.dot_general(rwt_ref[...], y2b, (((1,), (1,)), ((), ())), preferred_element_type=F32)
    m = jnp.max(logits, axis=0, keepdims=True)
    ex = jnp.exp(logits - m)
    probs = ex / jnp.sum(ex, axis=0, keepdims=True)
    sel = probs + rb_ref[...]
    srow = [sel[e:e + 1, :] for e in range(N_EXPERTS)]
    prow = [probs[e:e + 1, :] for e in range(N_EXPERTS)]

    def top2_sum(v0, v1, v2, v3):
        a, b = jnp.maximum(v0, v1), jnp.minimum(v0, v1)
        c, dd = jnp.maximum(v2, v3), jnp.minimum(v2, v3)
        return jnp.maximum(a, c) + jnp.maximum(jnp.minimum(a, c), jnp.maximum(b, dd))

    gscore = [top2_sum(*srow[4 * g:4 * g + 4]) for g in range(N_GROUPS)]
    best, grp = gscore[0], jnp.zeros((1, t), jnp.int32)
    for g in range(1, N_GROUPS):
        better = gscore[g] > best
        best = jnp.where(better, gscore[g], best)
        grp = jnp.where(better, g, grp)
    sg, pg = [], []
    for j in range(EXPERTS_PER_GROUP):
        s_j, p_j = srow[j], prow[j]
        for g in range(1, N_GROUPS):
            s_j = jnp.where(grp == g, srow[4 * g + j], s_j)
            p_j = jnp.where(grp == g, prow[4 * g + j], p_j)
        sg.append(s_j)
        pg.append(p_j)
    b0, i0 = sg[0], jnp.zeros((1, t), jnp.int32)
    for j in range(1, EXPERTS_PER_GROUP):
        better = sg[j] > b0
        b0 = jnp.where(better, sg[j], b0)
        i0 = jnp.where(better, j, i0)
    b1, i1 = jnp.full((1, t), -jnp.inf, F32), jnp.zeros((1, t), jnp.int32)
    for j in range(EXPERTS_PER_GROUP):
        better = (i0 != j) & (sg[j] > b1)
        b1 = jnp.where(better, sg[j], b1)
        i1 = jnp.where(better, j, i1)
    p0, p1 = jnp.zeros((1, t), F32), jnp.zeros((1, t), F32)
    for j in range(EXPERTS_PER_GROUP):
        p0 = jnp.where(i0 == j, pg[j], p0)
        p1 = jnp.where(i1 == j, pg[j], p1)
    e0 = grp * EXPERTS_PER_GROUP + i0
    e1 = grp * EXPERTS_PER_GROUP + i1
    den = p0 + p1
    g0, g1 = p0 / den, p1 / den

    eid = lax.broadcasted_iota(jnp.int32, (N_EXPERTS, t), 0)
    hit0, hit1 = eid == e0, eid == e1
    onehot = (hit0 | hit1).astype(BF16)
    src = lax.broadcasted_iota(jnp.int32, (t, t), 0)
    dst = lax.broadcasted_iota(jnp.int32, (t, t), 1)
    before = (src < dst).astype(BF16)
    excl = jnp.dot(onehot, before, preferred_element_type=F32) + carry_ref[...]
    r0 = jnp.sum(jnp.where(hit0, excl, 0.0), axis=0, keepdims=True)
    r1 = jnp.sum(jnp.where(hit1, excl, 0.0), axis=0, keepdims=True)
    carry_ref[...] += jnp.sum(onehot.astype(F32), axis=1, keepdims=True)
    zero = jnp.zeros((1, t), F32)
    return jnp.concatenate([e0.astype(F32), e1.astype(F32), r0, r1, g0, g1, zero, zero], axis=0)


def _mixer_kernel(kind, tiles_per_seq, *refs):
    h_ref, gmix_ref, gffn_ref, rwt_ref, rb_ref = refs[:5]
    n_w = {"gmlp": 5, "pool": 5, "conv": 6}[kind]
    w_refs = refs[5:5 + n_w]
    hmid_ref, y2_ref, route_ref, counts_ref = refs[5 + n_w:9 + n_w]
    scratch = refs[9 + n_w:]
    carry_ref = scratch[0]
    i = pl.program_id(0)
    t = h_ref.shape[0]
    pos0 = (i % tiles_per_seq) * t

    @pl.when(i == 0)
    def _():
        carry_ref[...] = jnp.zeros_like(carry_ref)

    h = h_ref[...]
    xn = _rms(h, gmix_ref[...]).astype(BF16)
    if kind == "gmlp":
        mix = _mixer_gmlp(xn, *w_refs)
    elif kind == "pool":
        mix = _mixer_pool(xn, pos0, *w_refs, *scratch[1:])
    else:
        mix = _mixer_conv(xn, pos0, *w_refs, *scratch[1:])
    hmid = h + mix
    hmid_ref[...] = hmid
    y2 = _rms(hmid, gffn_ref[...])
    y2_ref[...] = y2
    route_ref[...] = _route(y2.astype(BF16), rwt_ref, rb_ref, carry_ref)
    counts_ref[...] = jnp.broadcast_to(carry_ref[...], counts_ref.shape)


def _mixer_call(kind, h, gmix, gffn, rwt, rb, weights, seq_len):
    n, d = h.shape
    t = TOKEN_TILE
    tiles_per_seq = seq_len // t
    row = lambda i: (i, 0)
    in_specs = [pl.BlockSpec((t, d), row), _resident((1, d)), _resident((1, d)),
                _resident(rwt.shape), _resident(rb.shape)] + [_resident(w.shape) for w in weights]
    out_specs = [pl.BlockSpec((t, d), row), pl.BlockSpec((t, d), row),
                 pl.BlockSpec((ROUTE_ROWS, t), lambda i: (0, i)),
                 pl.BlockSpec((N_EXPERTS, 128), lambda i: (0, 0))]
    out_shape = [jax.ShapeDtypeStruct((n, d), F32), jax.ShapeDtypeStruct((n, d), F32),
                 jax.ShapeDtypeStruct((ROUTE_ROWS, n), F32), jax.ShapeDtypeStruct((N_EXPERTS, 128), F32)]
    scratch = [pltpu.VMEM((N_EXPERTS, 1), F32)]
    if kind == "pool":
        scratch.append(pltpu.VMEM((POOL_HALO, d), F32))
    elif kind == "conv":
        scratch += [pltpu.VMEM((CONV_HALO + t, d), F32), pltpu.VMEM((SUBLANES - 1, CONV_HALO + t, CONV_COLS), F32),
                    pltpu.VMEM((t, d), F32)]
    return pl.pallas_call(
        functools.partial(_mixer_kernel, kind, tiles_per_seq),
        grid=(n // t,), in_specs=in_specs, out_specs=out_specs, out_shape=out_shape,
        scratch_shapes=scratch, name=f"mixer_{kind}",
        compiler_params=pltpu.CompilerParams(dimension_semantics=("arbitrary",), vmem_limit_bytes=VMEM_LIMIT),
    )(h, gmix, gffn, rwt, rb, *weights)


def _row_copy(src_ref, src_row, dst_ref, dst_row, sem):
    return pltpu.make_async_copy(src_ref.at[pl.ds(src_row, 1), :], dst_ref.at[pl.ds(dst_row, 1), :], sem)


def _scatter_kernel(dest_ref, seg_ref, y_ref, xb_ref, zero_ref, sem, pad_sem):
    t = y_ref.shape[0]
    n = t * pl.num_programs(0)
    base = pl.program_id(0) * t
    n_seg = N_EXPERTS + 1

    @pl.when(pl.program_id(0) == 0)
    def _():
        zero_ref[...] = jnp.zeros_like(zero_ref)

        def per_expert(fn):
            def body(e, c):
                start = seg_ref[e]
                lax.fori_loop(start + seg_ref[n_seg + e], start + seg_ref[2 * n_seg + e],
                              lambda r, c2: (fn(_row_copy(zero_ref, 0, xb_ref, r, pad_sem)), c2)[1], 0)
                return c
            lax.fori_loop(0, n_seg, body, 0)

        per_expert(lambda cp: cp.start())
        per_expert(lambda cp: cp.wait())

    def issue(r, c):
        for k in range(TOP_K):
            _row_copy(y_ref, r, xb_ref, dest_ref[k * n + base + r], sem).start(priority=k)
        return c

    lax.fori_loop(0, t, issue, 0, unroll=8)
    for k in range(TOP_K):
        pltpu.make_async_copy(y_ref, xb_ref.at[pl.ds(0, t), :], sem).wait()


def _scatter_call(dest, seg, y2, n_rows):
    n, d = y2.shape
    t = TOKEN_TILE
    return pl.pallas_call(
        _scatter_kernel,
        grid_spec=pltpu.PrefetchScalarGridSpec(
            num_scalar_prefetch=2, grid=(n // t,),
            in_specs=[pl.BlockSpec((t, d), lambda i, dest, seg: (i, 0))],
            out_specs=pl.BlockSpec(memory_space=pl.ANY),
            scratch_shapes=[pltpu.VMEM((SUBLANES, d), F32), pltpu.SemaphoreType.DMA, pltpu.SemaphoreType.DMA]),
        out_shape=jax.ShapeDtypeStruct((n_rows, d), F32), name="moe_scatter",
        compiler_params=pltpu.CompilerParams(dimension_semantics=("arbitrary",)),
    )(dest, seg, y2)


def _expert_kernel(block_e_ref, n_used_ref, x_ref, wg_ref, wu_ref, wd_ref, o_ref):
    del block_e_ref

    @pl.when(pl.program_id(0) < n_used_ref[0])
    def _():
        x = x_ref[...]
        gate = jnp.dot(x, wg_ref[...], preferred_element_type=F32)
        up = jnp.dot(x, wu_ref[...], preferred_element_type=F32)
        o_ref[...] = jnp.dot(jax.nn.silu(gate) * up, wd_ref[...], preferred_element_type=F32)

    @pl.when(pl.program_id(0) >= n_used_ref[0])
    def _():
        o_ref[...] = jnp.zeros_like(o_ref)


def _expert_call(layer, block_e, n_used, xb, wg, wu, wd):
    n_rows, d = xb.shape
    f = wg.shape[-1]
    tm = EXPERT_TILE
    rows_in = lambda b, be, nu: (jnp.minimum(b, nu[0] - 1), 0)
    wmap = lambda b, be, nu: (layer, be[b], 0, 0)
    wspec = lambda shape: pl.BlockSpec((None, None) + shape, wmap, pipeline_mode=pl.Buffered(1))
    return pl.pallas_call(
        _expert_kernel,
        grid_spec=pltpu.PrefetchScalarGridSpec(
            num_scalar_prefetch=2, grid=(n_rows // tm,),
            in_specs=[pl.BlockSpec((tm, d), rows_in), wspec((d, f)), wspec((d, f)), wspec((f, d))],
            out_specs=pl.BlockSpec((tm, d), lambda b, be, nu: (b, 0))),
        out_shape=jax.ShapeDtypeStruct((n_rows, d), F32), name="moe_experts",
        compiler_params=pltpu.CompilerParams(dimension_semantics=("arbitrary",), vmem_limit_bytes=VMEM_LIMIT),
    )(block_e, n_used, xb, wg, wu, wd)


def _combine_kernel(final, dest_ref, h_ref, gate_ref, g_ref, yb_ref, o_ref, buf0, buf1, sem):
    t = h_ref.shape[0]
    n = t * pl.num_programs(0)
    base = pl.program_id(0) * t
    bufs = (buf0, buf1)

    def issue(r, c):
        for k in range(TOP_K):
            _row_copy(yb_ref, dest_ref[k * n + base + r], bufs[k], r, sem).start(priority=k)
        return c

    lax.fori_loop(0, t, issue, 0, unroll=8)
    for k in range(TOP_K):
        pltpu.make_async_copy(yb_ref.at[pl.ds(0, t), :], bufs[k], sem).wait()
    gates = gate_ref[...]
    out = h_ref[...] + gates[:, 0:1] * buf0[...] + gates[:, 1:2] * buf1[...]
    if final:
        out = _rms(out, g_ref[...])
    o_ref[...] = out


def _combine_call(final, dest, h, gates, g, yb):
    n, d = h.shape
    t = TOKEN_TILE
    row = lambda i, dest: (i, 0)
    return pl.pallas_call(
        functools.partial(_combine_kernel, final),
        grid_spec=pltpu.PrefetchScalarGridSpec(
            num_scalar_prefetch=1, grid=(n // t,),
            in_specs=[pl.BlockSpec((t, d), row), pl.BlockSpec((t, TOP_K), row),
                      pl.BlockSpec((1, d), lambda i, dest: (0, 0)), pl.BlockSpec(memory_space=pl.ANY)],
            out_specs=pl.BlockSpec((t, d), row),
            scratch_shapes=[pltpu.VMEM((t, d), F32), pltpu.VMEM((t, d), F32), pltpu.SemaphoreType.DMA]),
        out_shape=jax.ShapeDtypeStruct((n, d), F32), name="moe_combine",
        compiler_params=pltpu.CompilerParams(dimension_semantics=("arbitrary",)),
    )(dest, h, gates, g, yb)


def _plan(route, counts, n_rows):
    e = route[0:2].astype(jnp.int32)
    rank = route[2:4].astype(jnp.int32)
    gates = route[4:6].T
    cnt = counts[:, 0].astype(jnp.int32)
    pcnt = (cnt + EXPERT_TILE - 1) // EXPERT_TILE * EXPERT_TILE
    pends = jnp.cumsum(pcnt)
    pstarts = pends - pcnt
    eids = jnp.arange(N_EXPERTS, dtype=jnp.int32)
    start_of = jnp.sum(jnp.where(e[..., None] == eids, pstarts, 0), axis=-1)
    dest = (start_of + rank).reshape(-1)
    n_used = pends[-1] // EXPERT_TILE
    first_row = jnp.minimum(jnp.arange(n_rows // EXPERT_TILE, dtype=jnp.int32), n_used - 1) * EXPERT_TILE
    block_e = jnp.sum((pends[None, :] <= first_row[:, None]).astype(jnp.int32), axis=1)
    block_e = jnp.minimum(block_e, N_EXPERTS - 1)
    tail = n_rows - pends[-1:]
    seg = jnp.concatenate([pstarts, pends[-1:], cnt, 0 * tail, pcnt, tail]).astype(jnp.int32)
    return dest, seg, gates, block_e, n_used.astype(jnp.int32).reshape(1)


def kernel(x, norm_mix_g, norm_ffn_g, final_g, a_w_in, a_v_gain, a_w_s, a_b_s, a_w_out, b_w_in, b_w_grp, b_b_grp, b_scale, b_w_out, c_w_in, c_dw_w, c_dw_b, c_ln_g, c_ln_b, c_w_out, router_w, router_b, moe_w_gate, moe_w_up, moe_w_down):
    bsz, seq_len, d = x.shape
    depth = norm_mix_g.shape[0]
    n = bsz * seq_len
    n_rows = n * TOP_K + N_EXPERTS * EXPERT_TILE
    h = x.reshape(n, d)
    rwt = router_w.T.astype(BF16)
    rb = router_b.reshape(N_EXPERTS, 1)
    for i in range(depth):
        m, j = i % N_MIXERS, i // N_MIXERS
        if m == 0:
            kind = "gmlp"
            weights = (a_w_in[j].astype(BF16), a_v_gain[j].reshape(1, d), a_w_s[j],
                       a_b_s[j].reshape(A_HEADS, CHUNK, 1), a_w_out[j].astype(BF16))
        elif m == 1:
            kind = "pool"
            weights = (b_w_in[j].astype(BF16), b_w_grp[j].astype(BF16),
                       b_b_grp[j].reshape(B_GROUPS, 1, d // B_GROUPS), b_scale[j].reshape(1, d),
                       b_w_out[j].astype(BF16))
        else:
            kind = "conv"
            weights = (c_w_in[j].astype(BF16), c_dw_w[j], c_dw_b[j].reshape(1, d), c_ln_g[j].reshape(1, d),
                       c_ln_b[j].reshape(1, d), c_w_out[j].astype(BF16))
        hmid, y2, route, counts = _mixer_call(
            kind, h, norm_mix_g[i].reshape(1, d), norm_ffn_g[i].reshape(1, d), rwt, rb, weights, seq_len)
        dest, seg, gates, block_e, n_used = _plan(route, counts, n_rows)
        xb = _scatter_call(dest, seg, y2, n_rows)
        yb = _expert_call(i, block_e, n_used, xb, moe_w_gate, moe_w_up, moe_w_down)
        h = _combine_call(i == depth - 1, dest, hmid, gates, final_g.reshape(1, d), yb)
    return h.reshape(bsz, seq_len, d)
```

```python
import functools

import jax
import jax.numpy as jnp
from jax import lax
from jax.experimental import pallas as pl
from jax.experimental.pallas import tpu as pltpu

F32 = jnp.float32
BF16 = jnp.bfloat16

N_MIXERS = 3
CHUNK = 128
A_HEADS = 8
B_GROUPS = 4
POOL_WINDOWS = (2, 4, 8, 16)
CONV_WIDTH = 31
N_EXPERTS = 16
N_GROUPS = 4
EXPERTS_PER_GROUP = N_EXPERTS // N_GROUPS
TOP_K = 2
RMS_EPS = 1e-6
LN_EPS = 1e-5

SUBLANES = 8
POOL_HALO = 16
CONV_HALO = 32
CONV_COLS = 512
CONV_ROWS = 32

TOKEN_TILE = 256
EXPERT_TILE = 256
ROUTE_ROWS = 8
VMEM_LIMIT = 56 * 1024 * 1024


def _resident(shape):
    nd = len(shape)
    return pl.BlockSpec(shape, lambda *_: (0,) * nd, pipeline_mode=pl.Buffered(1))


def _rms(x, g):
    ms = jnp.mean(x * x, axis=-1, keepdims=True)
    return x * lax.rsqrt(ms + RMS_EPS) * g


def _mixer_gmlp(xn, w_in_ref, vg_ref, ws_ref, bs_ref, w_out_ref):
    t, d = xn.shape
    hd = d // A_HEADS
    v = jax.nn.gelu(jnp.dot(xn, w_in_ref[:, d:], preferred_element_type=F32))
    vb = _rms(v, vg_ref[...]).astype(BF16)
    row = lax.broadcasted_iota(jnp.int32, (CHUNK, CHUNK), 0)
    col = lax.broadcasted_iota(jnp.int32, (CHUNK, CHUNK), 1)
    causal = col <= row
    ws = [jnp.where(causal, ws_ref[h], 0.0).astype(BF16) for h in range(A_HEADS)]
    rows = []
    for c in range(t // CHUNK):
        cols = []
        for h in range(A_HEADS):
            blk = vb[c * CHUNK:(c + 1) * CHUNK, h * hd:(h + 1) * hd]
            cols.append(jnp.dot(ws[h], blk, preferred_element_type=F32) + bs_ref[h])
        rows.append(jnp.concatenate(cols, axis=1))
    sv = jnp.concatenate(rows, axis=0)
    u = jax.nn.gelu(jnp.dot(xn, w_in_ref[:, :d], preferred_element_type=F32))
    return jnp.dot((u * sv).astype(BF16), w_out_ref[...], preferred_element_type=F32)


def _mixer_pool(xn, pos0, w_in_ref, w_grp_ref, b_grp_ref, scale_ref, w_out_ref, halo_ref):
    t, d = xn.shape
    gd = d // B_GROUPS
    p = jnp.dot(xn, w_in_ref[...], preferred_element_type=F32)

    @pl.when(pos0 == 0)
    def _():
        halo_ref[...] = jnp.zeros_like(halo_ref)

    ext = jnp.concatenate([halo_ref[...], p], axis=0)
    halo_ref[...] = p[t - POOL_HALO:, :]
    pos = pos0 + lax.broadcasted_iota(jnp.int32, (t, 1), 0)
    zs = []
    for g, w in enumerate(POOL_WINDOWS):
        e = ext[:, g * gd:(g + 1) * gd]
        acc, s = e, 1
        while s < w:
            acc = acc[s:, :] + acc[:-s, :]
            s *= 2
        win = acc[POOL_HALO - w + 1:POOL_HALO - w + 1 + t, :]
        count = jnp.minimum(pos + 1, w).astype(F32)
        mix = win / count - p[:, g * gd:(g + 1) * gd]
        z = jnp.dot(mix.astype(BF16), w_grp_ref[g], preferred_element_type=F32) + b_grp_ref[g]
        zs.append(z)
    z = jnp.concatenate(zs, axis=1) * scale_ref[...]
    return jnp.dot(z.astype(BF16), w_out_ref[...], preferred_element_type=F32)


def _mixer_conv(xn, pos0, w_in_ref, dw_w_ref, dw_b_ref, ln_g_ref, ln_b_ref, w_out_ref, ext_ref, sh_ref, y_ref):
    t, d = xn.shape
    a = jnp.dot(xn, w_in_ref[:, :d], preferred_element_type=F32)
    gate = jnp.dot(xn, w_in_ref[:, d:], preferred_element_type=F32)

    @pl.when(pos0 == 0)
    def _():
        ext_ref[0:CONV_HALO, :] = jnp.zeros((CONV_HALO, d), F32)

    ext_ref[CONV_HALO:, :] = a * jax.nn.sigmoid(gate)
    off = CONV_HALO - (CONV_WIDTH - 1)
    span = t + CONV_HALO - SUBLANES
    for cb in range(d // CONV_COLS):
        cols = slice(cb * CONV_COLS, (cb + 1) * CONV_COLS)
        for q in range(1, SUBLANES):
            sh_ref[q - 1, 0:span, :] = ext_ref[q:q + span, cols]

        def row_block(j, c, cols=cols):
            r0 = pl.multiple_of(j * CONV_ROWS, CONV_ROWS)
            acc = jnp.broadcast_to(dw_b_ref[:, cols], (CONV_ROWS, CONV_COLS))
            for k in range(CONV_WIDTH):
                q, base = (off + k) % SUBLANES, r0 + (off + k) // SUBLANES * SUBLANES
                if q == 0:
                    tap = ext_ref[pl.ds(base, CONV_ROWS), cols]
                else:
                    tap = sh_ref[q - 1, pl.ds(base, CONV_ROWS), :]
                acc = acc + tap * dw_w_ref[k:k + 1, cols]
            y_ref[pl.ds(r0, CONV_ROWS), cols] = acc
            return c

        lax.fori_loop(0, t // CONV_ROWS, row_block, 0)
    ext_ref[0:CONV_HALO, :] = ext_ref[t:t + CONV_HALO, :]
    y = y_ref[...]
    mu = jnp.mean(y, axis=-1, keepdims=True)
    yc = y - mu
    var = jnp.mean(yc * yc, axis=-1, keepdims=True)
    yn = yc * lax.rsqrt(var + LN_EPS) * ln_g_ref[...] + ln_b_ref[...]
    return jnp.dot(jax.nn.silu(yn).astype(BF16), w_out_ref[...], preferred_element_type=F32)


def _route(y2b, rwt_ref, rb_ref, carry_ref):
    t = y2b.shape[0]
    logits = lax.dot_general(rwt_ref[...], y2b, (((1,), (1,)), ((), ())), preferred_element_type=F32)
    m = jnp.max(logits, axis=0, keepdims=True)
    ex = jnp.exp(logits - m)
    probs = ex / jnp.sum(ex, axis=0, keepdims=True)
    sel = probs + rb_ref[...]
    srow = [sel[e:e + 1, :] for e in range(N_EXPERTS)]
    prow = [probs[e:e + 1, :] for e in range(N_EXPERTS)]

    def top2_sum(v0, v1, v2, v3):
        a, b = jnp.maximum(v0, v1), jnp.minimum(v0, v1)
        c, dd = jnp.maximum(v2, v3), jnp.minimum(v2, v3)
        return jnp.maximum(a, c) + jnp.maximum(jnp.minimum(a, c), jnp.maximum(b, dd))

    gscore = [top2_sum(*srow[4 * g:4 * g + 4]) for g in range(N_GROUPS)]
    best, grp = gscore[0], jnp.zeros((1, t), jnp.int32)
    for g in range(1, N_GROUPS):
        better = gscore[g] > best
        best = jnp.where(better, gscore[g], best)
        grp = jnp.where(better, g, grp)
    sg, pg = [], []
    for j in range(EXPERTS_PER_GROUP):
        s_j, p_j = srow[j], prow[j]
        for g in range(1, N_GROUPS):
            s_j = jnp.where(grp == g, srow[4 * g + j], s_j)
            p_j = jnp.where(grp == g, prow[4 * g + j], p_j)
        sg.append(s_j)
        pg.append(p_j)
    b0, i0 = sg[0], jnp.zeros((1, t), jnp.int32)
    for j in range(1, EXPERTS_PER_GROUP):
        better = sg[j] > b0
        b0 = jnp.where(better, sg[j], b0)
        i0 = jnp.where(better, j, i0)
    b1, i1 = jnp.full((1, t), -jnp.inf, F32), jnp.zeros((1, t), jnp.int32)
    for j in range(EXPERTS_PER_GROUP):
        better = (i0 != j) & (sg[j] > b1)
        b1 = jnp.where(better, sg[j], b1)
        i1 = jnp.where(better, j, i1)
    p0, p1 = jnp.zeros((1, t), F32), jnp.zeros((1, t), F32)
    for j in range(EXPERTS_PER_GROUP):
        p0 = jnp.where(i0 == j, pg[j], p0)
        p1 = jnp.where(i1 == j, pg[j], p1)
    e0 = grp * EXPERTS_PER_GROUP + i0
    e1 = grp * EXPERTS_PER_GROUP + i1
    den = p0 + p1
    g0, g1 = p0 / den, p1 / den

    eid = lax.broadcasted_iota(jnp.int32, (N_EXPERTS, t), 0)
    hit0, hit1 = eid == e0, eid == e1
    onehot = (hit0 | hit1).astype(BF16)
    src = lax.broadcasted_iota(jnp.int32, (t, t), 0)
    dst = lax.broadcasted_iota(jnp.int32, (t, t), 1)
    before = (src < dst).astype(BF16)
    excl = jnp.dot(onehot, before, preferred_element_type=F32) + carry_ref[...]
    r0 = jnp.sum(jnp.where(hit0, excl, 0.0), axis=0, keepdims=True)
    r1 = jnp.sum(jnp.where(hit1, excl, 0.0), axis=0, keepdims=True)
    carry_ref[...] += jnp.sum(onehot.astype(F32), axis=1, keepdims=True)
    zero = jnp.zeros((1, t), F32)
    return jnp.concatenate([e0.astype(F32), e1.astype(F32), r0, r1, g0, g1, zero, zero], axis=0)


def _mixer_kernel(kind, tiles_per_seq, *refs):
    h_ref, gmix_ref, gffn_ref, rwt_ref, rb_ref = refs[:5]
    n_w = {"gmlp": 5, "pool": 5, "conv": 6}[kind]
    w_refs = refs[5:5 + n_w]
    hmid_ref, y2_ref, route_ref, counts_ref = refs[5 + n_w:9 + n_w]
    scratch = refs[9 + n_w:]
    carry_ref = scratch[0]
    i = pl.program_id(0)
    t = h_ref.shape[0]
    pos0 = (i % tiles_per_seq) * t

    @pl.when(i == 0)
    def _():
        carry_ref[...] = jnp.zeros_like(carry_ref)

    h = h_ref[...]
    xn = _rms(h, gmix_ref[...]).astype(BF16)
    if kind == "gmlp":
        mix = _mixer_gmlp(xn, *w_refs)
    elif kind == "pool":
        mix = _mixer_pool(xn, pos0, *w_refs, *scratch[1:])
    else:
        mix = _mixer_conv(xn, pos0, *w_refs, *scratch[1:])
    hmid = h + mix
    hmid_ref[...] = hmid
    y2 = _rms(hmid, gffn_ref[...])
    y2_ref[...] = y2
    route_ref[...] = _route(y2.astype(BF16), rwt_ref, rb_ref, carry_ref)
    counts_ref[...] = jnp.broadcast_to(carry_ref[...], counts_ref.shape)


def _mixer_call(kind, h, gmix, gffn, rwt, rb, weights, seq_len):
    n, d = h.shape
    t = TOKEN_TILE
    tiles_per_seq = seq_len // t
    row = lambda i: (i, 0)
    in_specs = [pl.BlockSpec((t, d), row), _resident((1, d)), _resident((1, d)),
                _resident(rwt.shape), _resident(rb.shape)] + [_resident(w.shape) for w in weights]
    out_specs = [pl.BlockSpec((t, d), row), pl.BlockSpec((t, d), row),
                 pl.BlockSpec((ROUTE_ROWS, t), lambda i: (0, i)),
                 pl.BlockSpec((N_EXPERTS, 128), lambda i: (0, 0))]
    out_shape = [jax.ShapeDtypeStruct((n, d), F32), jax.ShapeDtypeStruct((n, d), F32),
                 jax.ShapeDtypeStruct((ROUTE_ROWS, n), F32), jax.ShapeDtypeStruct((N_EXPERTS, 128), F32)]
    scratch = [pltpu.VMEM((N_EXPERTS, 1), F32)]
    if kind == "pool":
        scratch.append(pltpu.VMEM((POOL_HALO, d), F32))
    elif kind == "conv":
        scratch += [pltpu.VMEM((CONV_HALO + t, d), F32), pltpu.VMEM((SUBLANES - 1, CONV_HALO + t, CONV_COLS), F32),
                    pltpu.VMEM((t, d), F32)]
    return pl.pallas_call(
        functools.partial(_mixer_kernel, kind, tiles_per_seq),
        grid=(n // t,), in_specs=in_specs, out_specs=out_specs, out_shape=out_shape,
        scratch_shapes=scratch, name=f"mixer_{kind}",
        compiler_params=pltpu.CompilerParams(dimension_semantics=("arbitrary",), vmem_limit_bytes=VMEM_LIMIT),
    )(h, gmix, gffn, rwt, rb, *weights)


def _row_copy(src_ref, src_row, dst_ref, dst_row, sem):
    return pltpu.make_async_copy(src_ref.at[pl.ds(src_row, 1), :], dst_ref.at[pl.ds(dst_row, 1), :], sem)


def _scatter_kernel(dest_ref, seg_ref, y_ref, xb_ref, zero_ref, sem, pad_sem):
    t = y_ref.shape[0]
    n = t * pl.num_programs(0)
    base = pl.program_id(0) * t
    n_seg = N_EXPERTS + 1
    tb = zero_ref.shape[0]

    @pl.when(pl.program_id(0) == 0)
    def _():
        zero_ref[...] = jnp.zeros_like(zero_ref)

        def per_expert(fn):
            def body(e, c):
                start = seg_ref[e]
                lax.fori_loop(start + seg_ref[n_seg + e], start + seg_ref[2 * n_seg + e],
                              lambda r, c2: (fn(_row_copy(zero_ref, 0, xb_ref, r, pad_sem)), c2)[1], 0)
                return c
            lax.fori_loop(0, N_EXPERTS, body, 0)

        def tail_blocks(fn):
            def body(b, c):
                row = pl.multiple_of(seg_ref[N_EXPERTS] + b * tb, tb)
                fn(pltpu.make_async_copy(zero_ref, xb_ref.at[pl.ds(row, tb), :], pad_sem))
                return c
            lax.fori_loop(0, seg_ref[2 * n_seg + N_EXPERTS] // tb, body, 0)

        per_expert(lambda cp: cp.start())
        tail_blocks(lambda cp: cp.start())
        per_expert(lambda cp: cp.wait())
        tail_blocks(lambda cp: cp.wait())

    def issue(r, c):
        for k in range(TOP_K):
            _row_copy(y_ref, r, xb_ref, dest_ref[k * n + base + r], sem).start(priority=k)
        return c

    lax.fori_loop(0, t, issue, 0, unroll=8)
    for k in range(TOP_K):
        pltpu.make_async_copy(y_ref, xb_ref.at[pl.ds(0, t), :], sem).wait()


def _scatter_call(dest, seg, y2, n_rows):
    n, d = y2.shape
    t = TOKEN_TILE
    return pl.pallas_call(
        _scatter_kernel,
        grid_spec=pltpu.PrefetchScalarGridSpec(
            num_scalar_prefetch=2, grid=(n // t,),
            in_specs=[pl.BlockSpec((t, d), lambda i, dest, seg: (i, 0))],
            out_specs=pl.BlockSpec(memory_space=pl.ANY),
            scratch_shapes=[pltpu.VMEM((EXPERT_TILE, d), F32), pltpu.SemaphoreType.DMA, pltpu.SemaphoreType.DMA]),
        out_shape=jax.ShapeDtypeStruct((n_rows, d), F32), name="moe_scatter",
        compiler_params=pltpu.CompilerParams(dimension_semantics=("arbitrary",)),
    )(dest, seg, y2)


def _expert_kernel(block_e_ref, n_used_ref, x_ref, wg_a, wg_b, wu_a, wu_b, wd_a, wd_b, o_ref):
    del block_e_ref

    @pl.when(pl.program_id(0) < n_used_ref[0])
    def _():
        x = x_ref[...]

        def half(wg_ref, wu_ref, wd_ref):
            gate = jnp.dot(x, wg_ref[...], preferred_element_type=F32)
            up = jnp.dot(x, wu_ref[...], preferred_element_type=F32)
            return jnp.dot(jax.nn.silu(gate) * up, wd_ref[...], preferred_element_type=F32)

        o_ref[...] = half(wg_a, wu_a, wd_a) + half(wg_b, wu_b, wd_b)

    @pl.when(pl.program_id(0) >= n_used_ref[0])
    def _():
        o_ref[...] = jnp.zeros_like(o_ref)


def _expert_call(layer, block_e, n_used, xb, wg, wu, wd):
    n_rows, d = xb.shape
    fh = wg.shape[-1] // 2
    tm = EXPERT_TILE
    rows_in = lambda b, be, nu: (jnp.minimum(b, nu[0] - 1), 0)

    def wspec(shape, half, axis):
        imap = lambda b, be, nu: (layer, be[b]) + ((0, half) if axis == 1 else (half, 0))
        mode = {} if half == 0 else {"pipeline_mode": pl.Buffered(1)}
        return pl.BlockSpec((None, None) + shape, imap, **mode)

    w_specs = [wspec((d, fh), 0, 1), wspec((d, fh), 1, 1), wspec((d, fh), 0, 1), wspec((d, fh), 1, 1),
               wspec((fh, d), 0, 0), wspec((fh, d), 1, 0)]
    return pl.pallas_call(
        _expert_kernel,
        grid_spec=pltpu.PrefetchScalarGridSpec(
            num_scalar_prefetch=2, grid=(n_rows // tm,),
            in_specs=[pl.BlockSpec((tm, d), rows_in)] + w_specs,
            out_specs=pl.BlockSpec((tm, d), lambda b, be, nu: (b, 0))),
        out_shape=jax.ShapeDtypeStruct((n_rows, d), F32), name="moe_experts",
        compiler_params=pltpu.CompilerParams(dimension_semantics=("arbitrary",), vmem_limit_bytes=VMEM_LIMIT),
    )(block_e, n_used, xb, wg, wg, wu, wu, wd, wd)


def _combine_kernel(final, dest_ref, h_ref, gate_ref, g_ref, yb_ref, o_ref, buf_ref, sem):
    t = h_ref.shape[0]
    i = pl.program_id(0)
    n_tiles = pl.num_programs(0)
    n = t * n_tiles

    def issue_tile(tile, slot):
        def issue(r, c):
            for k in range(TOP_K):
                _row_copy(yb_ref, dest_ref[k * n + tile * t + r], buf_ref.at[slot, k], r,
                          sem.at[slot]).start(priority=k)
            return c

        lax.fori_loop(0, t, issue, 0, unroll=8)

    slot = i % 2

    @pl.when(i == 0)
    def _():
        issue_tile(0, 0)

    @pl.when(i + 1 < n_tiles)
    def _():
        issue_tile(i + 1, 1 - slot)

    for k in range(TOP_K):
        pltpu.make_async_copy(yb_ref.at[pl.ds(0, t), :], buf_ref.at[slot, k], sem.at[slot]).wait()
    gates = gate_ref[...]
    out = h_ref[...] + gates[:, 0:1] * buf_ref[slot, 0] + gates[:, 1:2] * buf_ref[slot, 1]
    if final:
        out = _rms(out, g_ref[...])
    o_ref[...] = out


def _combine_call(final, dest, h, gates, g, yb):
    n, d = h.shape
    t = TOKEN_TILE
    row = lambda i, dest: (i, 0)
    return pl.pallas_call(
        functools.partial(_combine_kernel, final),
        grid_spec=pltpu.PrefetchScalarGridSpec(
            num_scalar_prefetch=1, grid=(n // t,),
            in_specs=[pl.BlockSpec((t, d), row), pl.BlockSpec((t, TOP_K), row),
                      pl.BlockSpec((1, d), lambda i, dest: (0, 0)), pl.BlockSpec(memory_space=pl.ANY)],
            out_specs=pl.BlockSpec((t, d), row),
            scratch_shapes=[pltpu.VMEM((2, TOP_K, t, d), F32), pltpu.SemaphoreType.DMA((2,))]),
        out_shape=jax.ShapeDtypeStruct((n, d), F32), name="moe_combine",
        compiler_params=pltpu.CompilerParams(dimension_semantics=("arbitrary",)),
    )(dest, h, gates, g, yb)


def _plan(route, counts, n_rows):
    e = route[0:2].astype(jnp.int32)
    rank = route[2:4].astype(jnp.int32)
    gates = route[4:6].T
    cnt = counts[:, 0].astype(jnp.int32)
    pcnt = (cnt + EXPERT_TILE - 1) // EXPERT_TILE * EXPERT_TILE
    pends = jnp.cumsum(pcnt)
    pstarts = pends - pcnt
    eids = jnp.arange(N_EXPERTS, dtype=jnp.int32)
    start_of = jnp.sum(jnp.where(e[..., None] == eids, pstarts, 0), axis=-1)
    dest = (start_of + rank).reshape(-1)
    n_used = pends[-1] // EXPERT_TILE
    first_row = jnp.minimum(jnp.arange(n_rows // EXPERT_TILE, dtype=jnp.int32), n_used - 1) * EXPERT_TILE
    block_e = jnp.sum((pends[None, :] <= first_row[:, None]).astype(jnp.int32), axis=1)
    block_e = jnp.minimum(block_e, N_EXPERTS - 1)
    tail = n_rows - pends[-1:]
    seg = jnp.concatenate([pstarts, pends[-1:], cnt, 0 * tail, pcnt, tail]).astype(jnp.int32)
    return dest, seg, gates, block_e, n_used.astype(jnp.int32).reshape(1)


def kernel(x, norm_mix_g, norm_ffn_g, final_g, a_w_in, a_v_gain, a_w_s, a_b_s, a_w_out, b_w_in, b_w_grp, b_b_grp, b_scale, b_w_out, c_w_in, c_dw_w, c_dw_b, c_ln_g, c_ln_b, c_w_out, router_w, router_b, moe_w_gate, moe_w_up, moe_w_down):
    bsz, seq_len, d = x.shape
    depth = norm_mix_g.shape[0]
    n = bsz * seq_len
    n_rows = n * TOP_K + N_EXPERTS * EXPERT_TILE
    h = x.reshape(n, d)
    rwt = router_w.T.astype(BF16)
    rb = router_b.reshape(N_EXPERTS, 1)
    for i in range(depth):
        m, j = i % N_MIXERS, i // N_MIXERS
        if m == 0:
            kind = "gmlp"
            weights = (a_w_in[j].astype(BF16), a_v_gain[j].reshape(1, d), a_w_s[j],
                       a_b_s[j].reshape(A_HEADS, CHUNK, 1), a_w_out[j].astype(BF16))
        elif m == 1:
            kind = "pool"
            weights = (b_w_in[j].astype(BF16), b_w_grp[j].astype(BF16),
                       b_b_grp[j].reshape(B_GROUPS, 1, d // B_GROUPS), b_scale[j].reshape(1, d),
                       b_w_out[j].astype(BF16))
        else:
            kind = "conv"
            weights = (c_w_in[j].astype(BF16), c_dw_w[j], c_dw_b[j].reshape(1, d), c_ln_g[j].reshape(1, d),
                       c_ln_b[j].reshape(1, d), c_w_out[j].astype(BF16))
        hmid, y2, route, counts = _mixer_call(
            kind, h, norm_mix_g[i].reshape(1, d), norm_ffn_g[i].reshape(1, d), rwt, rb, weights, seq_len)
        dest, seg, gates, block_e, n_used = _plan(route, counts, n_rows)
        xb = _scatter_call(dest, seg, y2, n_rows)
        yb = _expert_call(i, block_e, n_used, xb, moe_w_gate, moe_w_up, moe_w_down)
        h = _combine_call(i == depth - 1, dest, hmid, gates, final_g.reshape(1, d), yb)
    return h.reshape(bsz, seq_len, d)
```

```python
import functools

import jax
import jax.numpy as jnp
from jax import lax
from jax.experimental import pallas as pl
from jax.experimental.pallas import tpu as pltpu

F32 = jnp.float32
BF16 = jnp.bfloat16

N_MIXERS = 3
CHUNK = 128
A_HEADS = 8
B_GROUPS = 4
POOL_WINDOWS = (2, 4, 8, 16)
CONV_WIDTH = 31
N_EXPERTS = 16
N_GROUPS = 4
EXPERTS_PER_GROUP = N_EXPERTS // N_GROUPS
TOP_K = 2
RMS_EPS = 1e-6
LN_EPS = 1e-5

SUBLANES = 8
POOL_HALO = 16
CONV_HALO = 32
CONV_COLS = 512
CONV_ROWS = 32

TOKEN_TILE = 256
EXPERT_TILE = 256
ROUTE_ROWS = 8
VMEM_LIMIT = 56 * 1024 * 1024
EXPERT_VMEM_LIMIT = 60 * 1024 * 1024


def _resident(shape):
    nd = len(shape)
    return pl.BlockSpec(shape, lambda *_: (0,) * nd, pipeline_mode=pl.Buffered(1))


def _rms(x, g):
    ms = jnp.mean(x * x, axis=-1, keepdims=True)
    return x * lax.rsqrt(ms + RMS_EPS) * g


def _mixer_gmlp(xn, w_in_ref, vg_ref, ws_ref, bs_ref, w_out_ref):
    t, d = xn.shape
    hd = d // A_HEADS
    v = jax.nn.gelu(jnp.dot(xn, w_in_ref[:, d:], preferred_element_type=F32))
    vb = _rms(v, vg_ref[...]).astype(BF16)
    row = lax.broadcasted_iota(jnp.int32, (CHUNK, CHUNK), 0)
    col = lax.broadcasted_iota(jnp.int32, (CHUNK, CHUNK), 1)
    causal = col <= row
    ws = [jnp.where(causal, ws_ref[h], 0.0).astype(BF16) for h in range(A_HEADS)]
    rows = []
    for c in range(t // CHUNK):
        cols = []
        for h in range(A_HEADS):
            blk = vb[c * CHUNK:(c + 1) * CHUNK, h * hd:(h + 1) * hd]
            cols.append(jnp.dot(ws[h], blk, preferred_element_type=F32) + bs_ref[h])
        rows.append(jnp.concatenate(cols, axis=1))
    sv = jnp.concatenate(rows, axis=0)
    u = jax.nn.gelu(jnp.dot(xn, w_in_ref[:, :d], preferred_element_type=F32))
    return jnp.dot((u * sv).astype(BF16), w_out_ref[...], preferred_element_type=F32)


def _mixer_pool(xn, pos0, w_in_ref, w_grp_ref, b_grp_ref, scale_ref, w_out_ref, halo_ref):
    t, d = xn.shape
    gd = d // B_GROUPS
    p = jnp.dot(xn, w_in_ref[...], preferred_element_type=F32)

    @pl.when(pos0 == 0)
    def _():
        halo_ref[...] = jnp.zeros_like(halo_ref)

    ext = jnp.concatenate([halo_ref[...], p], axis=0)
    halo_ref[...] = p[t - POOL_HALO:, :]
    pos = pos0 + lax.broadcasted_iota(jnp.int32, (t, 1), 0)
    zs = []
    for g, w in enumerate(POOL_WINDOWS):
        e = ext[:, g * gd:(g + 1) * gd]
        acc, s = e, 1
        while s < w:
            acc = acc[s:, :] + acc[:-s, :]
            s *= 2
        win = acc[POOL_HALO - w + 1:POOL_HALO - w + 1 + t, :]
        count = jnp.minimum(pos + 1, w).astype(F32)
        mix = win / count - p[:, g * gd:(g + 1) * gd]
        z = jnp.dot(mix.astype(BF16), w_grp_ref[g], preferred_element_type=F32) + b_grp_ref[g]
        zs.append(z)
    z = jnp.concatenate(zs, axis=1) * scale_ref[...]
    return jnp.dot(z.astype(BF16), w_out_ref[...], preferred_element_type=F32)


def _mixer_conv(xn, pos0, w_in_ref, dw_w_ref, dw_b_ref, ln_g_ref, ln_b_ref, w_out_ref, ext_ref, sh_ref, y_ref):
    t, d = xn.shape
    a = jnp.dot(xn, w_in_ref[:, :d], preferred_element_type=F32)
    gate = jnp.dot(xn, w_in_ref[:, d:], preferred_element_type=F32)

    @pl.when(pos0 == 0)
    def _():
        ext_ref[0:CONV_HALO, :] = jnp.zeros((CONV_HALO, d), F32)

    ext_ref[CONV_HALO:, :] = a * jax.nn.sigmoid(gate)
    off = CONV_HALO - (CONV_WIDTH - 1)
    span = t + CONV_HALO - SUBLANES
    for cb in range(d // CONV_COLS):
        cols = slice(cb * CONV_COLS, (cb + 1) * CONV_COLS)
        for q in range(1, SUBLANES):
            sh_ref[q - 1, 0:span, :] = ext_ref[q:q + span, cols]

        def row_block(j, c, cols=cols):
            r0 = pl.multiple_of(j * CONV_ROWS, CONV_ROWS)
            acc = jnp.broadcast_to(dw_b_ref[:, cols], (CONV_ROWS, CONV_COLS))
            for k in range(CONV_WIDTH):
                q, base = (off + k) % SUBLANES, r0 + (off + k) // SUBLANES * SUBLANES
                if q == 0:
                    tap = ext_ref[pl.ds(base, CONV_ROWS), cols]
                else:
                    tap = sh_ref[q - 1, pl.ds(base, CONV_ROWS), :]
                acc = acc + tap * dw_w_ref[k:k + 1, cols]
            y_ref[pl.ds(r0, CONV_ROWS), cols] = acc
            return c

        lax.fori_loop(0, t // CONV_ROWS, row_block, 0)
    ext_ref[0:CONV_HALO, :] = ext_ref[t:t + CONV_HALO, :]
    y = y_ref[...]
    mu = jnp.mean(y, axis=-1, keepdims=True)
    yc = y - mu
    var = jnp.mean(yc * yc, axis=-1, keepdims=True)
    yn = yc * lax.rsqrt(var + LN_EPS) * ln_g_ref[...] + ln_b_ref[...]
    return jnp.dot(jax.nn.silu(yn).astype(BF16), w_out_ref[...], preferred_element_type=F32)


def _route(y2b, rwt_ref, rb_ref, carry_ref):
    t = y2b.shape[0]
    logits = lax.dot_general(rwt_ref[...], y2b, (((1,), (1,)), ((), ())), preferred_element_type=F32)
    m = jnp.max(logits, axis=0, keepdims=True)
    ex = jnp.exp(logits - m)
    probs = ex / jnp.sum(ex, axis=0, keepdims=True)
    sel = probs + rb_ref[...]
    srow = [sel[e:e + 1, :] for e in range(N_EXPERTS)]
    prow = [probs[e:e + 1, :] for e in range(N_EXPERTS)]

    def top2_sum(v0, v1, v2, v3):
        a, b = jnp.maximum(v0, v1), jnp.minimum(v0, v1)
        c, dd = jnp.maximum(v2, v3), jnp.minimum(v2, v3)
        return jnp.maximum(a, c) + jnp.maximum(jnp.minimum(a, c), jnp.maximum(b, dd))

    gscore = [top2_sum(*srow[4 * g:4 * g + 4]) for g in range(N_GROUPS)]
    best, grp = gscore[0], jnp.zeros((1, t), jnp.int32)
    for g in range(1, N_GROUPS):
        better = gscore[g] > best
        best = jnp.where(better, gscore[g], best)
        grp = jnp.where(better, g, grp)
    sg, pg = [], []
    for j in range(EXPERTS_PER_GROUP):
        s_j, p_j = srow[j], prow[j]
        for g in range(1, N_GROUPS):
            s_j = jnp.where(grp == g, srow[4 * g + j], s_j)
            p_j = jnp.where(grp == g, prow[4 * g + j], p_j)
        sg.append(s_j)
        pg.append(p_j)
    b0, i0 = sg[0], jnp.zeros((1, t), jnp.int32)
    for j in range(1, EXPERTS_PER_GROUP):
        better = sg[j] > b0
        b0 = jnp.where(better, sg[j], b0)
        i0 = jnp.where(better, j, i0)
    b1, i1 = jnp.full((1, t), -jnp.inf, F32), jnp.zeros((1, t), jnp.int32)
    for j in range(EXPERTS_PER_GROUP):
        better = (i0 != j) & (sg[j] > b1)
        b1 = jnp.where(better, sg[j], b1)
        i1 = jnp.where(better, j, i1)
    p0, p1 = jnp.zeros((1, t), F32), jnp.zeros((1, t), F32)
    for j in range(EXPERTS_PER_GROUP):
        p0 = jnp.where(i0 == j, pg[j], p0)
        p1 = jnp.where(i1 == j, pg[j], p1)
    e0 = grp * EXPERTS_PER_GROUP + i0
    e1 = grp * EXPERTS_PER_GROUP + i1
    den = p0 + p1
    g0, g1 = p0 / den, p1 / den

    eid = lax.broadcasted_iota(jnp.int32, (N_EXPERTS, t), 0)
    hit0, hit1 = eid == e0, eid == e1
    onehot = (hit0 | hit1).astype(BF16)
    src = lax.broadcasted_iota(jnp.int32, (t, t), 0)
    dst = lax.broadcasted_iota(jnp.int32, (t, t), 1)
    before = (src < dst).astype(BF16)
    excl = jnp.dot(onehot, before, preferred_element_type=F32) + carry_ref[...]
    r0 = jnp.sum(jnp.where(hit0, excl, 0.0), axis=0, keepdims=True)
    r1 = jnp.sum(jnp.where(hit1, excl, 0.0), axis=0, keepdims=True)
    carry_ref[...] += jnp.sum(onehot.astype(F32), axis=1, keepdims=True)
    zero = jnp.zeros((1, t), F32)
    return jnp.concatenate([e0.astype(F32), e1.astype(F32), r0, r1, g0, g1, zero, zero], axis=0)


def _mixer_kernel(kind, tiles_per_seq, *refs):
    h_ref, gmix_ref, gffn_ref, rwt_ref, rb_ref = refs[:5]
    n_w = {"gmlp": 5, "pool": 5, "conv": 6}[kind]
    w_refs = refs[5:5 + n_w]
    hmid_ref, y2_ref, route_ref, counts_ref = refs[5 + n_w:9 + n_w]
    scratch = refs[9 + n_w:]
    carry_ref = scratch[0]
    i = pl.program_id(0)
    t = h_ref.shape[0]
    pos0 = (i % tiles_per_seq) * t

    @pl.when(i == 0)
    def _():
        carry_ref[...] = jnp.zeros_like(carry_ref)

    h = h_ref[...]
    xn = _rms(h, gmix_ref[...]).astype(BF16)
    if kind == "gmlp":
        mix = _mixer_gmlp(xn, *w_refs)
    elif kind == "pool":
        mix = _mixer_pool(xn, pos0, *w_refs, *scratch[1:])
    else:
        mix = _mixer_conv(xn, pos0, *w_refs, *scratch[1:])
    hmid = h + mix
    hmid_ref[...] = hmid
    y2 = _rms(hmid, gffn_ref[...])
    y2_ref[...] = y2
    route_ref[...] = _route(y2.astype(BF16), rwt_ref, rb_ref, carry_ref)
    counts_ref[...] = jnp.broadcast_to(carry_ref[...], counts_ref.shape)


def _mixer_call(kind, h, gmix, gffn, rwt, rb, weights, seq_len):
    n, d = h.shape
    t = TOKEN_TILE
    tiles_per_seq = seq_len // t
    row = lambda i: (i, 0)
    in_specs = [pl.BlockSpec((t, d), row), _resident((1, d)), _resident((1, d)),
                _resident(rwt.shape), _resident(rb.shape)] + [_resident(w.shape) for w in weights]
    out_specs = [pl.BlockSpec((t, d), row), pl.BlockSpec((t, d), row),
                 pl.BlockSpec((ROUTE_ROWS, t), lambda i: (0, i)),
                 pl.BlockSpec((N_EXPERTS, 128), lambda i: (0, 0))]
    out_shape = [jax.ShapeDtypeStruct((n, d), F32), jax.ShapeDtypeStruct((n, d), F32),
                 jax.ShapeDtypeStruct((ROUTE_ROWS, n), F32), jax.ShapeDtypeStruct((N_EXPERTS, 128), F32)]
    scratch = [pltpu.VMEM((N_EXPERTS, 1), F32)]
    if kind == "pool":
        scratch.append(pltpu.VMEM((POOL_HALO, d), F32))
    elif kind == "conv":
        scratch += [pltpu.VMEM((CONV_HALO + t, d), F32), pltpu.VMEM((SUBLANES - 1, CONV_HALO + t, CONV_COLS), F32),
                    pltpu.VMEM((t, d), F32)]
    return pl.pallas_call(
        functools.partial(_mixer_kernel, kind, tiles_per_seq),
        grid=(n // t,), in_specs=in_specs, out_specs=out_specs, out_shape=out_shape,
        scratch_shapes=scratch, name=f"mixer_{kind}",
        compiler_params=pltpu.CompilerParams(dimension_semantics=("arbitrary",), vmem_limit_bytes=VMEM_LIMIT),
    )(h, gmix, gffn, rwt, rb, *weights)


def _row_copy(src_ref, src_row, dst_ref, dst_row, sem):
    return pltpu.make_async_copy(src_ref.at[pl.ds(src_row, 1), :], dst_ref.at[pl.ds(dst_row, 1), :], sem)


def _scatter_kernel(dest_ref, seg_ref, y_ref, xb_ref, zero_ref, sem, pad_sem):
    t = y_ref.shape[0]
    n = t * pl.num_programs(0)
    base = pl.program_id(0) * t
    n_seg = N_EXPERTS + 1
    tb = zero_ref.shape[0]

    @pl.when(pl.program_id(0) == 0)
    def _():
        zero_ref[...] = jnp.zeros_like(zero_ref)

        def per_expert(fn):
            def body(e, c):
                start = seg_ref[e]
                lax.fori_loop(start + seg_ref[n_seg + e], start + seg_ref[2 * n_seg + e],
                              lambda r, c2: (fn(_row_copy(zero_ref, 0, xb_ref, r, pad_sem)), c2)[1], 0)
                return c
            lax.fori_loop(0, N_EXPERTS, body, 0)

        def tail_blocks(fn):
            def body(b, c):
                row = pl.multiple_of(seg_ref[N_EXPERTS] + b * tb, tb)
                fn(pltpu.make_async_copy(zero_ref, xb_ref.at[pl.ds(row, tb), :], pad_sem))
                return c
            lax.fori_loop(0, seg_ref[2 * n_seg + N_EXPERTS] // tb, body, 0)

        per_expert(lambda cp: cp.start())
        tail_blocks(lambda cp: cp.start())
        per_expert(lambda cp: cp.wait())
        tail_blocks(lambda cp: cp.wait())

    for r in range(t):
        for k in range(TOP_K):
            _row_copy(y_ref, r, xb_ref, dest_ref[k * n + base + r], sem).start(priority=k)
    for k in range(TOP_K):
        pltpu.make_async_copy(y_ref, xb_ref.at[pl.ds(0, t), :], sem).wait()


def _scatter_call(dest, seg, y2, n_rows):
    n, d = y2.shape
    t = TOKEN_TILE
    return pl.pallas_call(
        _scatter_kernel,
        grid_spec=pltpu.PrefetchScalarGridSpec(
            num_scalar_prefetch=2, grid=(n // t,),
            in_specs=[pl.BlockSpec((t, d), lambda i, dest, seg: (i, 0))],
            out_specs=pl.BlockSpec(memory_space=pl.ANY),
            scratch_shapes=[pltpu.VMEM((EXPERT_TILE, d), F32), pltpu.SemaphoreType.DMA, pltpu.SemaphoreType.DMA]),
        out_shape=jax.ShapeDtypeStruct((n_rows, d), F32), name="moe_scatter",
        compiler_params=pltpu.CompilerParams(dimension_semantics=("arbitrary",)),
    )(dest, seg, y2)


def _expert_kernel(block_e_ref, n_used_ref, x_ref, wg_ref, wu_ref, wd_ref, o_ref):
    del block_e_ref

    @pl.when(pl.program_id(0) < n_used_ref[0])
    def _():
        x = x_ref[...]
        gate = jnp.dot(x, wg_ref[...], preferred_element_type=F32)
        up = jnp.dot(x, wu_ref[...], preferred_element_type=F32)
        o_ref[...] = jnp.dot(jax.nn.silu(gate) * up, wd_ref[...], preferred_element_type=F32)

    @pl.when(pl.program_id(0) >= n_used_ref[0])
    def _():
        o_ref[...] = jnp.zeros_like(o_ref)


def _expert_call(layer, block_e, n_used, xb, wg, wu, wd):
    n_rows, d = xb.shape
    f = wg.shape[-1]
    tm = EXPERT_TILE
    rows_in = lambda b, be, nu: (jnp.minimum(b, nu[0] - 1), 0)
    wmap = lambda b, be, nu: (layer, be[b], 0, 0)
    wspec = lambda shape: pl.BlockSpec((None, None) + shape, wmap)
    return pl.pallas_call(
        _expert_kernel,
        grid_spec=pltpu.PrefetchScalarGridSpec(
            num_scalar_prefetch=2, grid=(n_rows // tm,),
            in_specs=[pl.BlockSpec((tm, d), rows_in), wspec((d, f)), wspec((d, f)), wspec((f, d))],
            out_specs=pl.BlockSpec((tm, d), lambda b, be, nu: (b, 0))),
        out_shape=jax.ShapeDtypeStruct((n_rows, d), F32), name="moe_experts",
        compiler_params=pltpu.CompilerParams(dimension_semantics=("arbitrary",),
                                             vmem_limit_bytes=EXPERT_VMEM_LIMIT),
    )(block_e, n_used, xb, wg, wu, wd)


def _combine_kernel(final, dest_ref, h_ref, gate_ref, g_ref, yb_ref, o_ref, buf_ref, sem):
    t = h_ref.shape[0]
    i = pl.program_id(0)
    n_tiles = pl.num_programs(0)
    n = t * n_tiles

    def issue_tile(tile, slot):
        for r in range(t):
            for k in range(TOP_K):
                _row_copy(yb_ref, dest_ref[k * n + tile * t + r], buf_ref.at[slot, k], r,
                          sem.at[slot]).start(priority=k)

    slot = i % 2

    @pl.when(i == 0)
    def _():
        issue_tile(0, 0)

    @pl.when(i + 1 < n_tiles)
    def _():
        issue_tile(i + 1, 1 - slot)

    for k in range(TOP_K):
        pltpu.make_async_copy(yb_ref.at[pl.ds(0, t), :], buf_ref.at[slot, k], sem.at[slot]).wait()
    gates = gate_ref[...]
    out = h_ref[...] + gates[:, 0:1] * buf_ref[slot, 0] + gates[:, 1:2] * buf_ref[slot, 1]
    if final:
        out = _rms(out, g_ref[...])
    o_ref[...] = out


def _combine_call(final, dest, h, gates, g, yb):
    n, d = h.shape
    t = TOKEN_TILE
    row = lambda i, dest: (i, 0)
    return pl.pallas_call(
        functools.partial(_combine_kernel, final),
        grid_spec=pltpu.PrefetchScalarGridSpec(
            num_scalar_prefetch=1, grid=(n // t,),
            in_specs=[pl.BlockSpec((t, d), row), pl.BlockSpec((t, TOP_K), row),
                      pl.BlockSpec((1, d), lambda i, dest: (0, 0)), pl.BlockSpec(memory_space=pl.ANY)],
            out_specs=pl.BlockSpec((t, d), row),
            scratch_shapes=[pltpu.VMEM((2, TOP_K, t, d), F32), pltpu.SemaphoreType.DMA((2,))]),
        out_shape=jax.ShapeDtypeStruct((n, d), F32), name="moe_combine",
        compiler_params=pltpu.CompilerParams(dimension_semantics=("arbitrary",)),
    )(dest, h, gates, g, yb)


def _plan(route, counts, n_rows):
    e = route[0:2].astype(jnp.int32)
    rank = route[2:4].astype(jnp.int32)
    gates = route[4:6].T
    cnt = counts[:, 0].astype(jnp.int32)
    pcnt = (cnt + EXPERT_TILE - 1) // EXPERT_TILE * EXPERT_TILE
    pends = jnp.cumsum(pcnt)
    pstarts = pends - pcnt
    eids = jnp.arange(N_EXPERTS, dtype=jnp.int32)
    start_of = jnp.sum(jnp.where(e[..., None] == eids, pstarts, 0), axis=-1)
    dest = (start_of + rank).reshape(-1)
    n_used = pends[-1] // EXPERT_TILE
    first_row = jnp.minimum(jnp.arange(n_rows // EXPERT_TILE, dtype=jnp.int32), n_used - 1) * EXPERT_TILE
    block_e = jnp.sum((pends[None, :] <= first_row[:, None]).astype(jnp.int32), axis=1)
    block_e = jnp.minimum(block_e, N_EXPERTS - 1)
    tail = n_rows - pends[-1:]
    seg = jnp.concatenate([pstarts, pends[-1:], cnt, 0 * tail, pcnt, tail]).astype(jnp.int32)
    return dest, seg, gates, block_e, n_used.astype(jnp.int32).reshape(1)


def kernel(x, norm_mix_g, norm_ffn_g, final_g, a_w_in, a_v_gain, a_w_s, a_b_s, a_w_out, b_w_in, b_w_grp, b_b_grp, b_scale, b_w_out, c_w_in, c_dw_w, c_dw_b, c_ln_g, c_ln_b, c_w_out, router_w, router_b, moe_w_gate, moe_w_up, moe_w_down):
    bsz, seq_len, d = x.shape
    depth = norm_mix_g.shape[0]
    n = bsz * seq_len
    n_rows = n * TOP_K + N_EXPERTS * EXPERT_TILE
    h = x.reshape(n, d)
    rwt = router_w.T.astype(BF16)
    rb = router_b.reshape(N_EXPERTS, 1)
    for i in range(depth):
        m, j = i % N_MIXERS, i // N_MIXERS
        if m == 0:
            kind = "gmlp"
            weights = (a_w_in[j].astype(BF16), a_v_gain[j].reshape(1, d), a_w_s[j],
                       a_b_s[j].reshape(A_HEADS, CHUNK, 1), a_w_out[j].astype(BF16))
        elif m == 1:
            kind = "pool"
            weights = (b_w_in[j].astype(BF16), b_w_grp[j].astype(BF16),
                       b_b_grp[j].reshape(B_GROUPS, 1, d // B_GROUPS), b_scale[j].reshape(1, d),
                       b_w_out[j].astype(BF16))
        else:
            kind = "conv"
            weights = (c_w_in[j].astype(BF16), c_dw_w[j], c_dw_b[j].reshape(1, d), c_ln_g[j].reshape(1, d),
                       c_ln_b[j].reshape(1, d), c_w_out[j].astype(BF16))
        hmid, y2, route, counts = _mixer_call(
            kind, h, norm_mix_g[i].reshape(1, d), norm_ffn_g[i].reshape(1, d), rwt, rb, weights, seq_len)
        dest, seg, gates, block_e, n_used = _plan(route, counts, n_rows)
        xb = _scatter_call(dest, seg, y2, n_rows)
        yb = _expert_call(i, block_e, n_used, xb, moe_w_gate, moe_w_up, moe_w_down)
        h = _combine_call(i == depth - 1, dest, hmid, gates, final_g.reshape(1, d), yb)
    return h.reshape(bsz, seq_len, d)
```

```python
import functools

import jax
import jax.numpy as jnp
from jax import lax
from jax.experimental import pallas as pl
from jax.experimental.pallas import tpu as pltpu

F32 = jnp.float32
BF16 = jnp.bfloat16

N_MIXERS = 3
CHUNK = 128
A_HEADS = 8
B_GROUPS = 4
POOL_WINDOWS = (2, 4, 8, 16)
CONV_WIDTH = 31
N_EXPERTS = 16
N_GROUPS = 4
EXPERTS_PER_GROUP = N_EXPERTS // N_GROUPS
TOP_K = 2
RMS_EPS = 1e-6
LN_EPS = 1e-5

SUBLANES = 8
POOL_HALO = 16
CONV_HALO = 32
CONV_COLS = 256
CONV_ROWS = 64

TOKEN_TILE = 256
EXPERT_TILE = 256
ROUTE_ROWS = 8
VMEM_LIMIT = 56 * 1024 * 1024
EXPERT_VMEM_LIMIT = 60 * 1024 * 1024


def _resident(shape):
    nd = len(shape)
    return pl.BlockSpec(shape, lambda *_: (0,) * nd, pipeline_mode=pl.Buffered(1))


def _rms(x, g):
    ms = jnp.mean(x * x, axis=-1, keepdims=True)
    return x * lax.rsqrt(ms + RMS_EPS) * g


def _mixer_gmlp(xn, w_in_ref, vg_ref, ws_ref, bs_ref, w_out_ref):
    t, d = xn.shape
    hd = d // A_HEADS
    v = jax.nn.gelu(jnp.dot(xn, w_in_ref[:, d:], preferred_element_type=F32))
    vb = _rms(v, vg_ref[...]).astype(BF16)
    row = lax.broadcasted_iota(jnp.int32, (CHUNK, CHUNK), 0)
    col = lax.broadcasted_iota(jnp.int32, (CHUNK, CHUNK), 1)
    causal = col <= row
    ws = [jnp.where(causal, ws_ref[h], 0.0).astype(BF16) for h in range(A_HEADS)]
    rows = []
    for c in range(t // CHUNK):
        cols = []
        for h in range(A_HEADS):
            blk = vb[c * CHUNK:(c + 1) * CHUNK, h * hd:(h + 1) * hd]
            cols.append(jnp.dot(ws[h], blk, preferred_element_type=F32) + bs_ref[h])
        rows.append(jnp.concatenate(cols, axis=1))
    sv = jnp.concatenate(rows, axis=0)
    u = jax.nn.gelu(jnp.dot(xn, w_in_ref[:, :d], preferred_element_type=F32))
    return jnp.dot((u * sv).astype(BF16), w_out_ref[...], preferred_element_type=F32)


def _mixer_pool(xn, pos0, w_in_ref, w_grp_ref, b_grp_ref, scale_ref, w_out_ref, halo_ref):
    t, d = xn.shape
    gd = d // B_GROUPS
    p = jnp.dot(xn, w_in_ref[...], preferred_element_type=F32)

    @pl.when(pos0 == 0)
    def _():
        halo_ref[...] = jnp.zeros_like(halo_ref)

    ext = jnp.concatenate([halo_ref[...], p], axis=0)
    halo_ref[...] = p[t - POOL_HALO:, :]
    pos = pos0 + lax.broadcasted_iota(jnp.int32, (t, 1), 0)
    zs = []
    for g, w in enumerate(POOL_WINDOWS):
        e = ext[:, g * gd:(g + 1) * gd]
        acc, s = e, 1
        while s < w:
            acc = acc[s:, :] + acc[:-s, :]
            s *= 2
        win = acc[POOL_HALO - w + 1:POOL_HALO - w + 1 + t, :]
        count = jnp.minimum(pos + 1, w).astype(F32)
        mix = win / count - p[:, g * gd:(g + 1) * gd]
        z = jnp.dot(mix.astype(BF16), w_grp_ref[g], preferred_element_type=F32) + b_grp_ref[g]
        zs.append(z)
    z = jnp.concatenate(zs, axis=1) * scale_ref[...]
    return jnp.dot(z.astype(BF16), w_out_ref[...], preferred_element_type=F32)


def _mixer_conv(xn, pos0, w_in_ref, dw_w_ref, dw_b_ref, ln_g_ref, ln_b_ref, w_out_ref, ext_ref, sh_ref, y_ref,
                wb_ref):
    t, d = xn.shape
    a = jnp.dot(xn, w_in_ref[:, :d], preferred_element_type=F32)
    gate = jnp.dot(xn, w_in_ref[:, d:], preferred_element_type=F32)

    @pl.when(pos0 == 0)
    def _():
        ext_ref[0:CONV_HALO, :] = jnp.zeros((CONV_HALO, d), F32)
        for k in range(CONV_WIDTH):
            wb_ref[k] = jnp.broadcast_to(dw_w_ref[k:k + 1, :], (SUBLANES, d))

    ext_ref[CONV_HALO:, :] = a * jax.nn.sigmoid(gate)
    off = CONV_HALO - (CONV_WIDTH - 1)
    span = t + CONV_HALO - SUBLANES
    groups = CONV_ROWS // SUBLANES
    for cb in range(d // CONV_COLS):
        cols = slice(cb * CONV_COLS, (cb + 1) * CONV_COLS)
        for q in range(1, SUBLANES):
            sh_ref[q - 1, 0:span, :] = ext_ref[q:q + span, cols]

        def row_block(j, c, cols=cols):
            r0 = pl.multiple_of(j * CONV_ROWS, CONV_ROWS)
            accs = [jnp.broadcast_to(dw_b_ref[:, cols], (SUBLANES, CONV_COLS))] * groups
            for k in range(CONV_WIDTH):
                q, base = (off + k) % SUBLANES, r0 + (off + k) // SUBLANES * SUBLANES
                if q == 0:
                    tap = ext_ref[pl.ds(base, CONV_ROWS), cols]
                else:
                    tap = sh_ref[q - 1, pl.ds(base, CONV_ROWS), :]
                w8 = wb_ref[k, :, cols]
                accs = [acc + tap[g * SUBLANES:(g + 1) * SUBLANES, :] * w8 for g, acc in enumerate(accs)]
            y_ref[pl.ds(r0, CONV_ROWS), cols] = jnp.concatenate(accs, axis=0)
            return c

        lax.fori_loop(0, t // CONV_ROWS, row_block, 0)
    ext_ref[0:CONV_HALO, :] = ext_ref[t:t + CONV_HALO, :]
    y = y_ref[...]
    mu = jnp.mean(y, axis=-1, keepdims=True)
    yc = y - mu
    var = jnp.mean(yc * yc, axis=-1, keepdims=True)
    yn = yc * lax.rsqrt(var + LN_EPS) * ln_g_ref[...] + ln_b_ref[...]
    return jnp.dot(jax.nn.silu(yn).astype(BF16), w_out_ref[...], preferred_element_type=F32)


def _route(y2b, rwt_ref, rb_ref, carry_ref):
    t = y2b.shape[0]
    logits = lax.dot_general(rwt_ref[...], y2b, (((1,), (1,)), ((), ())), preferred_element_type=F32)
    m = jnp.max(logits, axis=0, keepdims=True)
    ex = jnp.exp(logits - m)
    probs = ex / jnp.sum(ex, axis=0, keepdims=True)
    sel = probs + rb_ref[...]
    srow = [sel[e:e + 1, :] for e in range(N_EXPERTS)]
    prow = [probs[e:e + 1, :] for e in range(N_EXPERTS)]

    def top2_sum(v0, v1, v2, v3):
        a, b = jnp.maximum(v0, v1), jnp.minimum(v0, v1)
        c, dd = jnp.maximum(v2, v3), jnp.minimum(v2, v3)
        return jnp.maximum(a, c) + jnp.maximum(jnp.minimum(a, c), jnp.maximum(b, dd))

    gscore = [top2_sum(*srow[4 * g:4 * g + 4]) for g in range(N_GROUPS)]
    best, grp = gscore[0], jnp.zeros((1, t), jnp.int32)
    for g in range(1, N_GROUPS):
        better = gscore[g] > best
        best = jnp.where(better, gscore[g], best)
        grp = jnp.where(better, g, grp)
    sg, pg = [], []
    for j in range(EXPERTS_PER_GROUP):
        s_j, p_j = srow[j], prow[j]
        for g in range(1, N_GROUPS):
            s_j = jnp.where(grp == g, srow[4 * g + j], s_j)
            p_j = jnp.where(grp == g, prow[4 * g + j], p_j)
        sg.append(s_j)
        pg.append(p_j)
    b0, i0 = sg[0], jnp.zeros((1, t), jnp.int32)
    for j in range(1, EXPERTS_PER_GROUP):
        better = sg[j] > b0
        b0 = jnp.where(better, sg[j], b0)
        i0 = jnp.where(better, j, i0)
    b1, i1 = jnp.full((1, t), -jnp.inf, F32), jnp.zeros((1, t), jnp.int32)
    for j in range(EXPERTS_PER_GROUP):
        better = (i0 != j) & (sg[j] > b1)
        b1 = jnp.where(better, sg[j], b1)
        i1 = jnp.where(better, j, i1)
    p0, p1 = jnp.zeros((1, t), F32), jnp.zeros((1, t), F32)
    for j in range(EXPERTS_PER_GROUP):
        p0 = jnp.where(i0 == j, pg[j], p0)
        p1 = jnp.where(i1 == j, pg[j], p1)
    e0 = grp * EXPERTS_PER_GROUP + i0
    e1 = grp * EXPERTS_PER_GROUP + i1
    den = p0 + p1
    g0, g1 = p0 / den, p1 / den

    eid = lax.broadcasted_iota(jnp.int32, (N_EXPERTS, t), 0)
    hit0, hit1 = eid == e0, eid == e1
    onehot = (hit0 | hit1).astype(BF16)
    src = lax.broadcasted_iota(jnp.int32, (t, t), 0)
    dst = lax.broadcasted_iota(jnp.int32, (t, t), 1)
    before = (src < dst).astype(BF16)
    excl = jnp.dot(onehot, before, preferred_element_type=F32) + carry_ref[...]
    r0 = jnp.sum(jnp.where(hit0, excl, 0.0), axis=0, keepdims=True)
    r1 = jnp.sum(jnp.where(hit1, excl, 0.0), axis=0, keepdims=True)
    carry_ref[...] += jnp.sum(onehot.astype(F32), axis=1, keepdims=True)
    zero = jnp.zeros((1, t), F32)
    return jnp.concatenate([e0.astype(F32), e1.astype(F32), r0, r1, g0, g1, zero, zero], axis=0)


def _mixer_kernel(kind, tiles_per_seq, *refs):
    h_ref, gmix_ref, gffn_ref, rwt_ref, rb_ref = refs[:5]
    n_w = {"gmlp": 5, "pool": 5, "conv": 6}[kind]
    w_refs = refs[5:5 + n_w]
    hmid_ref, y2_ref, route_ref, counts_ref = refs[5 + n_w:9 + n_w]
    scratch = refs[9 + n_w:]
    carry_ref = scratch[0]
    i = pl.program_id(0)
    t = h_ref.shape[0]
    pos0 = (i % tiles_per_seq) * t

    @pl.when(i == 0)
    def _():
        carry_ref[...] = jnp.zeros_like(carry_ref)

    h = h_ref[...]
    xn = _rms(h, gmix_ref[...]).astype(BF16)
    if kind == "gmlp":
        mix = _mixer_gmlp(xn, *w_refs)
    elif kind == "pool":
        mix = _mixer_pool(xn, pos0, *w_refs, *scratch[1:])
    else:
        mix = _mixer_conv(xn, pos0, *w_refs, *scratch[1:])
    hmid = h + mix
    hmid_ref[...] = hmid
    y2 = _rms(hmid, gffn_ref[...])
    y2_ref[...] = y2
    route_ref[...] = _route(y2.astype(BF16), rwt_ref, rb_ref, carry_ref)
    counts_ref[...] = jnp.broadcast_to(carry_ref[...], counts_ref.shape)


def _mixer_call(kind, h, gmix, gffn, rwt, rb, weights, seq_len):
    n, d = h.shape
    t = TOKEN_TILE
    tiles_per_seq = seq_len // t
    row = lambda i: (i, 0)
    in_specs = [pl.BlockSpec((t, d), row), _resident((1, d)), _resident((1, d)),
                _resident(rwt.shape), _resident(rb.shape)] + [_resident(w.shape) for w in weights]
    out_specs = [pl.BlockSpec((t, d), row), pl.BlockSpec((t, d), row),
                 pl.BlockSpec((ROUTE_ROWS, t), lambda i: (0, i)),
                 pl.BlockSpec((N_EXPERTS, 128), lambda i: (0, 0))]
    out_shape = [jax.ShapeDtypeStruct((n, d), F32), jax.ShapeDtypeStruct((n, d), F32),
                 jax.ShapeDtypeStruct((ROUTE_ROWS, n), F32), jax.ShapeDtypeStruct((N_EXPERTS, 128), F32)]
    scratch = [pltpu.VMEM((N_EXPERTS, 1), F32)]
    if kind == "pool":
        scratch.append(pltpu.VMEM((POOL_HALO, d), F32))
    elif kind == "conv":
        scratch += [pltpu.VMEM((CONV_HALO + t, d), F32), pltpu.VMEM((SUBLANES - 1, CONV_HALO + t, CONV_COLS), F32),
                    pltpu.VMEM((t, d), F32), pltpu.VMEM((CONV_WIDTH, SUBLANES, d), F32)]
    return pl.pallas_call(
        functools.partial(_mixer_kernel, kind, tiles_per_seq),
        grid=(n // t,), in_specs=in_specs, out_specs=out_specs, out_shape=out_shape,
        scratch_shapes=scratch, name=f"mixer_{kind}",
        compiler_params=pltpu.CompilerParams(dimension_semantics=("arbitrary",), vmem_limit_bytes=VMEM_LIMIT),
    )(h, gmix, gffn, rwt, rb, *weights)


def _row_copy(src_ref, src_row, dst_ref, dst_row, sem):
    return pltpu.make_async_copy(src_ref.at[pl.ds(src_row, 1), :], dst_ref.at[pl.ds(dst_row, 1), :], sem)


def _scatter_kernel(dest_ref, seg_ref, y_ref, xb_ref, zero_ref, sem, pad_sem):
    t = y_ref.shape[0]
    n = t * pl.num_programs(0)
    base = pl.program_id(0) * t
    n_seg = N_EXPERTS + 1
    tb = zero_ref.shape[0]

    @pl.when(pl.program_id(0) == 0)
    def _():
        zero_ref[...] = jnp.zeros_like(zero_ref)

        def per_expert(fn):
            def body(e, c):
                start = seg_ref[e]
                lax.fori_loop(start + seg_ref[n_seg + e], start + seg_ref[2 * n_seg + e],
                              lambda r, c2: (fn(_row_copy(zero_ref, 0, xb_ref, r, pad_sem)), c2)[1], 0)
                return c
            lax.fori_loop(0, N_EXPERTS, body, 0)

        def tail_blocks(fn):
            def body(b, c):
                row = pl.multiple_of(seg_ref[N_EXPERTS] + b * tb, tb)
                fn(pltpu.make_async_copy(zero_ref, xb_ref.at[pl.ds(row, tb), :], pad_sem))
                return c
            lax.fori_loop(0, seg_ref[2 * n_seg + N_EXPERTS] // tb, body, 0)

        per_expert(lambda cp: cp.start())
        tail_blocks(lambda cp: cp.start())
        per_expert(lambda cp: cp.wait())
        tail_blocks(lambda cp: cp.wait())

    for r in range(t):
        for k in range(TOP_K):
            _row_copy(y_ref, r, xb_ref, dest_ref[k * n + base + r], sem).start(priority=k)
    for k in range(TOP_K):
        pltpu.make_async_copy(y_ref, xb_ref.at[pl.ds(0, t), :], sem).wait()


def _scatter_call(dest, seg, y2, n_rows):
    n, d = y2.shape
    t = TOKEN_TILE
    return pl.pallas_call(
        _scatter_kernel,
        grid_spec=pltpu.PrefetchScalarGridSpec(
            num_scalar_prefetch=2, grid=(n // t,),
            in_specs=[pl.BlockSpec((t, d), lambda i, dest, seg: (i, 0))],
            out_specs=pl.BlockSpec(memory_space=pl.ANY),
            scratch_shapes=[pltpu.VMEM((EXPERT_TILE, d), F32), pltpu.SemaphoreType.DMA, pltpu.SemaphoreType.DMA]),
        out_shape=jax.ShapeDtypeStruct((n_rows, d), F32), name="moe_scatter",
        compiler_params=pltpu.CompilerParams(dimension_semantics=("arbitrary",)),
    )(dest, seg, y2)


def _expert_kernel(block_e_ref, n_used_ref, x_ref, wg_ref, wu_ref, wd_ref, o_ref):
    del block_e_ref

    @pl.when(pl.program_id(0) < n_used_ref[0])
    def _():
        x = x_ref[...]
        gate = jnp.dot(x, wg_ref[...], preferred_element_type=F32)
        up = jnp.dot(x, wu_ref[...], preferred_element_type=F32)
        o_ref[...] = jnp.dot(jax.nn.silu(gate) * up, wd_ref[...], preferred_element_type=F32)

    @pl.when(pl.program_id(0) >= n_used_ref[0])
    def _():
        o_ref[...] = jnp.zeros_like(o_ref)


def _expert_call(layer, block_e, n_used, xb, wg, wu, wd):
    n_rows, d = xb.shape
    f = wg.shape[-1]
    tm = EXPERT_TILE
    rows_in = lambda b, be, nu: (jnp.minimum(b, nu[0] - 1), 0)
    wmap = lambda b, be, nu: (layer, be[b], 0, 0)
    wspec = lambda shape: pl.BlockSpec((None, None) + shape, wmap)
    return pl.pallas_call(
        _expert_kernel,
        grid_spec=pltpu.PrefetchScalarGridSpec(
            num_scalar_prefetch=2, grid=(n_rows // tm,),
            in_specs=[pl.BlockSpec((tm, d), rows_in), wspec((d, f)), wspec((d, f)), wspec((f, d))],
            out_specs=pl.BlockSpec((tm, d), lambda b, be, nu: (b, 0))),
        out_shape=jax.ShapeDtypeStruct((n_rows, d), F32), name="moe_experts",
        compiler_params=pltpu.CompilerParams(dimension_semantics=("arbitrary",),
                                             vmem_limit_bytes=EXPERT_VMEM_LIMIT),
    )(block_e, n_used, xb, wg, wu, wd)


def _combine_kernel(final, dest_ref, h_ref, gate_ref, g_ref, yb_ref, o_ref, buf_ref, sem):
    t = h_ref.shape[0]
    i = pl.program_id(0)
    n_tiles = pl.num_programs(0)
    n = t * n_tiles

    def issue_tile(tile, slot):
        for r in range(t):
            for k in range(TOP_K):
                _row_copy(yb_ref, dest_ref[k * n + tile * t + r], buf_ref.at[slot, k], r,
                          sem.at[slot]).start(priority=k)

    slot = i % 2

    @pl.when(i == 0)
    def _():
        issue_tile(0, 0)

    @pl.when(i + 1 < n_tiles)
    def _():
        issue_tile(i + 1, 1 - slot)

    for k in range(TOP_K):
        pltpu.make_async_copy(yb_ref.at[pl.ds(0, t), :], buf_ref.at[slot, k], sem.at[slot]).wait()
    gates = gate_ref[...]
    out = h_ref[...] + gates[:, 0:1] * buf_ref[slot, 0] + gates[:, 1:2] * buf_ref[slot, 1]
    if final:
        out = _rms(out, g_ref[...])
    o_ref[...] = out


def _combine_call(final, dest, h, gates, g, yb):
    n, d = h.shape
    t = TOKEN_TILE
    row = lambda i, dest: (i, 0)
    return pl.pallas_call(
        functools.partial(_combine_kernel, final),
        grid_spec=pltpu.PrefetchScalarGridSpec(
            num_scalar_prefetch=1, grid=(n // t,),
            in_specs=[pl.BlockSpec((t, d), row), pl.BlockSpec((t, TOP_K), row),
                      pl.BlockSpec((1, d), lambda i, dest: (0, 0)), pl.BlockSpec(memory_space=pl.ANY)],
            out_specs=pl.BlockSpec((t, d), row),
            scratch_shapes=[pltpu.VMEM((2, TOP_K, t, d), F32), pltpu.SemaphoreType.DMA((2,))]),
        out_shape=jax.ShapeDtypeStruct((n, d), F32), name="moe_combine",
        compiler_params=pltpu.CompilerParams(dimension_semantics=("arbitrary",)),
    )(dest, h, gates, g, yb)


def _plan(route, counts, n_rows):
    e = route[0:2].astype(jnp.int32)
    rank = route[2:4].astype(jnp.int32)
    gates = route[4:6].T
    cnt = counts[:, 0].astype(jnp.int32)
    pcnt = (cnt + EXPERT_TILE - 1) // EXPERT_TILE * EXPERT_TILE
    pends = jnp.cumsum(pcnt)
    pstarts = pends - pcnt
    eids = jnp.arange(N_EXPERTS, dtype=jnp.int32)
    start_of = jnp.sum(jnp.where(e[..., None] == eids, pstarts, 0), axis=-1)
    dest = (start_of + rank).reshape(-1)
    n_used = pends[-1] // EXPERT_TILE
    first_row = jnp.minimum(jnp.arange(n_rows // EXPERT_TILE, dtype=jnp.int32), n_used - 1) * EXPERT_TILE
    block_e = jnp.sum((pends[None, :] <= first_row[:, None]).astype(jnp.int32), axis=1)
    block_e = jnp.minimum(block_e, N_EXPERTS - 1)
    tail = n_rows - pends[-1:]
    seg = jnp.concatenate([pstarts, pends[-1:], cnt, 0 * tail, pcnt, tail]).astype(jnp.int32)
    return dest, seg, gates, block_e, n_used.astype(jnp.int32).reshape(1)


def kernel(x, norm_mix_g, norm_ffn_g, final_g, a_w_in, a_v_gain, a_w_s, a_b_s, a_w_out, b_w_in, b_w_grp, b_b_grp, b_scale, b_w_out, c_w_in, c_dw_w, c_dw_b, c_ln_g, c_ln_b, c_w_out, router_w, router_b, moe_w_gate, moe_w_up, moe_w_down):
    bsz, seq_len, d = x.shape
    depth = norm_mix_g.shape[0]
    n = bsz * seq_len
    n_rows = n * TOP_K + N_EXPERTS * EXPERT_TILE
    h = x.reshape(n, d)
    rwt = router_w.T.astype(BF16)
    rb = router_b.reshape(N_EXPERTS, 1)
    for i in range(depth):
        m, j = i % N_MIXERS, i // N_MIXERS
        if m == 0:
            kind = "gmlp"
            weights = (a_w_in[j].astype(BF16), a_v_gain[j].reshape(1, d), a_w_s[j],
                       a_b_s[j].reshape(A_HEADS, CHUNK, 1), a_w_out[j].astype(BF16))
        elif m == 1:
            kind = "pool"
            weights = (b_w_in[j].astype(BF16), b_w_grp[j].astype(BF16),
                       b_b_grp[j].reshape(B_GROUPS, 1, d // B_GROUPS), b_scale[j].reshape(1, d),
                       b_w_out[j].astype(BF16))
        else:
            kind = "conv"
            weights = (c_w_in[j].astype(BF16), c_dw_w[j], c_dw_b[j].reshape(1, d), c_ln_g[j].reshape(1, d),
                       c_ln_b[j].reshape(1, d), c_w_out[j].astype(BF16))
        hmid, y2, route, counts = _mixer_call(
            kind, h, norm_mix_g[i].reshape(1, d), norm_ffn_g[i].reshape(1, d), rwt, rb, weights, seq_len)
        dest, seg, gates, block_e, n_used = _plan(route, counts, n_rows)
        xb = _scatter_call(dest, seg, y2, n_rows)
        yb = _expert_call(i, block_e, n_used, xb, moe_w_gate, moe_w_up, moe_w_down)
        h = _combine_call(i == depth - 1, dest, hmid, gates, final_g.reshape(1, d), yb)
    return h.reshape(bsz, seq_len, d)
```

```python
import functools

import jax
import jax.numpy as jnp
from jax import lax
from jax.experimental import pallas as pl
from jax.experimental.pallas import tpu as pltpu

F32 = jnp.float32
BF16 = jnp.bfloat16

N_MIXERS = 3
CHUNK = 128
A_HEADS = 8
B_GROUPS = 4
POOL_WINDOWS = (2, 4, 8, 16)
CONV_WIDTH = 31
N_EXPERTS = 16
N_GROUPS = 4
EXPERTS_PER_GROUP = N_EXPERTS // N_GROUPS
TOP_K = 2
RMS_EPS = 1e-6
LN_EPS = 1e-5

SUBLANES = 8
POOL_HALO = 16
CONV_HALO = 32
CONV_COLS = 128
CONV_ROWS = 128

TOKEN_TILE = 256
EXPERT_TILE = 256
ROUTE_ROWS = 8
VMEM_LIMIT = 56 * 1024 * 1024
EXPERT_VMEM_LIMIT = 60 * 1024 * 1024


def _resident(shape):
    nd = len(shape)
    return pl.BlockSpec(shape, lambda *_: (0,) * nd, pipeline_mode=pl.Buffered(1))


def _rms(x, g):
    ms = jnp.mean(x * x, axis=-1, keepdims=True)
    return x * lax.rsqrt(ms + RMS_EPS) * g


def _mixer_gmlp(xn, w_in_ref, vg_ref, ws_ref, bs_ref, w_out_ref):
    t, d = xn.shape
    hd = d // A_HEADS
    v = jax.nn.gelu(jnp.dot(xn, w_in_ref[:, d:], preferred_element_type=F32))
    vb = _rms(v, vg_ref[...]).astype(BF16)
    row = lax.broadcasted_iota(jnp.int32, (CHUNK, CHUNK), 0)
    col = lax.broadcasted_iota(jnp.int32, (CHUNK, CHUNK), 1)
    causal = col <= row
    ws = [jnp.where(causal, ws_ref[h], 0.0).astype(BF16) for h in range(A_HEADS)]
    rows = []
    for c in range(t // CHUNK):
        cols = []
        for h in range(A_HEADS):
            blk = vb[c * CHUNK:(c + 1) * CHUNK, h * hd:(h + 1) * hd]
            cols.append(jnp.dot(ws[h], blk, preferred_element_type=F32) + bs_ref[h])
        rows.append(jnp.concatenate(cols, axis=1))
    sv = jnp.concatenate(rows, axis=0)
    u = jax.nn.gelu(jnp.dot(xn, w_in_ref[:, :d], preferred_element_type=F32))
    return jnp.dot((u * sv).astype(BF16), w_out_ref[...], preferred_element_type=F32)


def _mixer_pool(xn, pos0, w_in_ref, w_grp_ref, b_grp_ref, scale_ref, w_out_ref, halo_ref):
    t, d = xn.shape
    gd = d // B_GROUPS
    p = jnp.dot(xn, w_in_ref[...], preferred_element_type=F32)

    @pl.when(pos0 == 0)
    def _():
        halo_ref[...] = jnp.zeros_like(halo_ref)

    ext = jnp.concatenate([halo_ref[...], p], axis=0)
    halo_ref[...] = p[t - POOL_HALO:, :]
    pos = pos0 + lax.broadcasted_iota(jnp.int32, (t, 1), 0)
    zs = []
    for g, w in enumerate(POOL_WINDOWS):
        e = ext[:, g * gd:(g + 1) * gd]
        acc, s = e, 1
        while s < w:
            acc = acc[s:, :] + acc[:-s, :]
            s *= 2
        win = acc[POOL_HALO - w + 1:POOL_HALO - w + 1 + t, :]
        count = jnp.minimum(pos + 1, w).astype(F32)
        mix = win / count - p[:, g * gd:(g + 1) * gd]
        z = jnp.dot(mix.astype(BF16), w_grp_ref[g], preferred_element_type=F32) + b_grp_ref[g]
        zs.append(z)
    z = jnp.concatenate(zs, axis=1) * scale_ref[...]
    return jnp.dot(z.astype(BF16), w_out_ref[...], preferred_element_type=F32)


def _mixer_conv(xn, pos0, w_in_ref, dw_w_ref, dw_b_ref, ln_g_ref, ln_b_ref, w_out_ref, ext_ref, sh_ref, y_ref,
                wb_ref):
    t, d = xn.shape
    a = jnp.dot(xn, w_in_ref[:, :d], preferred_element_type=F32)
    gate = jnp.dot(xn, w_in_ref[:, d:], preferred_element_type=F32)

    @pl.when(pos0 == 0)
    def _():
        ext_ref[0:CONV_HALO, :] = jnp.zeros((CONV_HALO, d), F32)
        for k in range(CONV_WIDTH):
            wb_ref[k] = jnp.broadcast_to(dw_w_ref[k:k + 1, :], (SUBLANES, d))

    ext_ref[CONV_HALO:, :] = a * jax.nn.sigmoid(gate)
    off = CONV_HALO - (CONV_WIDTH - 1)
    span = t + CONV_HALO - SUBLANES
    groups = CONV_ROWS // SUBLANES
    for cb in range(d // CONV_COLS):
        cols = slice(cb * CONV_COLS, (cb + 1) * CONV_COLS)
        for q in range(1, SUBLANES):
            sh_ref[q - 1, 0:span, :] = ext_ref[q:q + span, cols]

        def row_block(j, c, cols=cols):
            r0 = pl.multiple_of(j * CONV_ROWS, CONV_ROWS)
            accs = [jnp.broadcast_to(dw_b_ref[:, cols], (SUBLANES, CONV_COLS))] * groups
            for k in range(CONV_WIDTH):
                q, base = (off + k) % SUBLANES, r0 + (off + k) // SUBLANES * SUBLANES
                if q == 0:
                    tap = ext_ref[pl.ds(base, CONV_ROWS), cols]
                else:
                    tap = sh_ref[q - 1, pl.ds(base, CONV_ROWS), :]
                w8 = wb_ref[k, :, cols]
                accs = [acc + tap[g * SUBLANES:(g + 1) * SUBLANES, :] * w8 for g, acc in enumerate(accs)]
            y_ref[pl.ds(r0, CONV_ROWS), cols] = jnp.concatenate(accs, axis=0)
            return c

        lax.fori_loop(0, t // CONV_ROWS, row_block, 0)
    ext_ref[0:CONV_HALO, :] = ext_ref[t:t + CONV_HALO, :]
    y = y_ref[...]
    mu = jnp.mean(y, axis=-1, keepdims=True)
    yc = y - mu
    var = jnp.mean(yc * yc, axis=-1, keepdims=True)
    yn = yc * lax.rsqrt(var + LN_EPS) * ln_g_ref[...] + ln_b_ref[...]
    return jnp.dot(jax.nn.silu(yn).astype(BF16), w_out_ref[...], preferred_element_type=F32)


def _route(y2b, rwt_ref, rb_ref, carry_ref):
    t = y2b.shape[0]
    logits = lax.dot_general(rwt_ref[...], y2b, (((1,), (1,)), ((), ())), preferred_element_type=F32)
    m = jnp.max(logits, axis=0, keepdims=True)
    ex = jnp.exp(logits - m)
    probs = ex / jnp.sum(ex, axis=0, keepdims=True)
    sel = probs + rb_ref[...]
    srow = [sel[e:e + 1, :] for e in range(N_EXPERTS)]
    prow = [probs[e:e + 1, :] for e in range(N_EXPERTS)]

    def top2_sum(v0, v1, v2, v3):
        a, b = jnp.maximum(v0, v1), jnp.minimum(v0, v1)
        c, dd = jnp.maximum(v2, v3), jnp.minimum(v2, v3)
        return jnp.maximum(a, c) + jnp.maximum(jnp.minimum(a, c), jnp.maximum(b, dd))

    gscore = [top2_sum(*srow[4 * g:4 * g + 4]) for g in range(N_GROUPS)]
    best, grp = gscore[0], jnp.zeros((1, t), jnp.int32)
    for g in range(1, N_GROUPS):
        better = gscore[g] > best
        best = jnp.where(better, gscore[g], best)
        grp = jnp.where(better, g, grp)
    sg, pg = [], []
    for j in range(EXPERTS_PER_GROUP):
        s_j, p_j = srow[j], prow[j]
        for g in range(1, N_GROUPS):
            s_j = jnp.where(grp == g, srow[4 * g + j], s_j)
            p_j = jnp.where(grp == g, prow[4 * g + j], p_j)
        sg.append(s_j)
        pg.append(p_j)
    b0, i0 = sg[0], jnp.zeros((1, t), jnp.int32)
    for j in range(1, EXPERTS_PER_GROUP):
        better = sg[j] > b0
        b0 = jnp.where(better, sg[j], b0)
        i0 = jnp.where(better, j, i0)
    b1, i1 = jnp.full((1, t), -jnp.inf, F32), jnp.zeros((1, t), jnp.int32)
    for j in range(EXPERTS_PER_GROUP):
        better = (i0 != j) & (sg[j] > b1)
        b1 = jnp.where(better, sg[j], b1)
        i1 = jnp.where(better, j, i1)
    p0, p1 = jnp.zeros((1, t), F32), jnp.zeros((1, t), F32)
    for j in range(EXPERTS_PER_GROUP):
        p0 = jnp.where(i0 == j, pg[j], p0)
        p1 = jnp.where(i1 == j, pg[j], p1)
    e0 = grp * EXPERTS_PER_GROUP + i0
    e1 = grp * EXPERTS_PER_GROUP + i1
    den = p0 + p1
    g0, g1 = p0 / den, p1 / den

    eid = lax.broadcasted_iota(jnp.int32, (N_EXPERTS, t), 0)
    hit0, hit1 = eid == e0, eid == e1
    onehot = (hit0 | hit1).astype(BF16)
    src = lax.broadcasted_iota(jnp.int32, (t, t), 0)
    dst = lax.broadcasted_iota(jnp.int32, (t, t), 1)
    before = (src < dst).astype(BF16)
    excl = jnp.dot(onehot, before, preferred_element_type=F32) + carry_ref[...]
    r0 = jnp.sum(jnp.where(hit0, excl, 0.0), axis=0, keepdims=True)
    r1 = jnp.sum(jnp.where(hit1, excl, 0.0), axis=0, keepdims=True)
    carry_ref[...] += jnp.sum(onehot.astype(F32), axis=1, keepdims=True)
    zero = jnp.zeros((1, t), F32)
    return jnp.concatenate([e0.astype(F32), e1.astype(F32), r0, r1, g0, g1, zero, zero], axis=0)


def _mixer_kernel(kind, tiles_per_seq, *refs):
    h_ref, gmix_ref, gffn_ref, rwt_ref, rb_ref = refs[:5]
    n_w = {"gmlp": 5, "pool": 5, "conv": 6}[kind]
    w_refs = refs[5:5 + n_w]
    hmid_ref, y2_ref, route_ref, counts_ref = refs[5 + n_w:9 + n_w]
    scratch = refs[9 + n_w:]
    carry_ref = scratch[0]
    i = pl.program_id(0)
    t = h_ref.shape[0]
    pos0 = (i % tiles_per_seq) * t

    @pl.when(i == 0)
    def _():
        carry_ref[...] = jnp.zeros_like(carry_ref)

    h = h_ref[...]
    xn = _rms(h, gmix_ref[...]).astype(BF16)
    if kind == "gmlp":
        mix = _mixer_gmlp(xn, *w_refs)
    elif kind == "pool":
        mix = _mixer_pool(xn, pos0, *w_refs, *scratch[1:])
    else:
        mix = _mixer_conv(xn, pos0, *w_refs, *scratch[1:])
    hmid = h + mix
    hmid_ref[...] = hmid
    y2 = _rms(hmid, gffn_ref[...])
    y2_ref[...] = y2
    route_ref[...] = _route(y2.astype(BF16), rwt_ref, rb_ref, carry_ref)
    counts_ref[...] = jnp.broadcast_to(carry_ref[...], counts_ref.shape)


def _mixer_call(kind, h, gmix, gffn, rwt, rb, weights, seq_len):
    n, d = h.shape
    t = TOKEN_TILE
    tiles_per_seq = seq_len // t
    row = lambda i: (i, 0)
    in_specs = [pl.BlockSpec((t, d), row), _resident((1, d)), _resident((1, d)),
                _resident(rwt.shape), _resident(rb.shape)] + [_resident(w.shape) for w in weights]
    out_specs = [pl.BlockSpec((t, d), row), pl.BlockSpec((t, d), row),
                 pl.BlockSpec((ROUTE_ROWS, t), lambda i: (0, i)),
                 pl.BlockSpec((N_EXPERTS, 128), lambda i: (0, 0))]
    out_shape = [jax.ShapeDtypeStruct((n, d), F32), jax.ShapeDtypeStruct((n, d), F32),
                 jax.ShapeDtypeStruct((ROUTE_ROWS, n), F32), jax.ShapeDtypeStruct((N_EXPERTS, 128), F32)]
    scratch = [pltpu.VMEM((N_EXPERTS, 1), F32)]
    if kind == "pool":
        scratch.append(pltpu.VMEM((POOL_HALO, d), F32))
    elif kind == "conv":
        scratch += [pltpu.VMEM((CONV_HALO + t, d), F32), pltpu.VMEM((SUBLANES - 1, CONV_HALO + t, CONV_COLS), F32),
                    pltpu.VMEM((t, d), F32), pltpu.VMEM((CONV_WIDTH, SUBLANES, d), F32)]
    return pl.pallas_call(
        functools.partial(_mixer_kernel, kind, tiles_per_seq),
        grid=(n // t,), in_specs=in_specs, out_specs=out_specs, out_shape=out_shape,
        scratch_shapes=scratch, name=f"mixer_{kind}",
        compiler_params=pltpu.CompilerParams(dimension_semantics=("arbitrary",), vmem_limit_bytes=VMEM_LIMIT),
    )(h, gmix, gffn, rwt, rb, *weights)


def _row_copy(src_ref, src_row, dst_ref, dst_row, sem):
    return pltpu.make_async_copy(src_ref.at[pl.ds(src_row, 1), :], dst_ref.at[pl.ds(dst_row, 1), :], sem)


def _scatter_kernel(dest_ref, seg_ref, y_ref, xb_ref, zero_ref, sem, pad_sem):
    t = y_ref.shape[0]
    n = t * pl.num_programs(0)
    base = pl.program_id(0) * t
    n_seg = N_EXPERTS + 1
    tb = zero_ref.shape[0]

    @pl.when(pl.program_id(0) == 0)
    def _():
        zero_ref[...] = jnp.zeros_like(zero_ref)

        def per_expert(fn):
            def body(e, c):
                start = seg_ref[e]
                lax.fori_loop(start + seg_ref[n_seg + e], start + seg_ref[2 * n_seg + e],
                              lambda r, c2: (fn(_row_copy(zero_ref, 0, xb_ref, r, pad_sem)), c2)[1], 0)
                return c
            lax.fori_loop(0, N_EXPERTS, body, 0)

        def tail_blocks(fn):
            def body(b, c):
                row = pl.multiple_of(seg_ref[N_EXPERTS] + b * tb, tb)
                fn(pltpu.make_async_copy(zero_ref, xb_ref.at[pl.ds(row, tb), :], pad_sem))
                return c
            lax.fori_loop(0, seg_ref[2 * n_seg + N_EXPERTS] // tb, body, 0)

        per_expert(lambda cp: cp.start())
        tail_blocks(lambda cp: cp.start())
        per_expert(lambda cp: cp.wait())
        tail_blocks(lambda cp: cp.wait())

    for r in range(t):
        for k in range(TOP_K):
            _row_copy(y_ref, r, xb_ref, dest_ref[k * n + base + r], sem).start(priority=k)
    for k in range(TOP_K):
        pltpu.make_async_copy(y_ref, xb_ref.at[pl.ds(0, t), :], sem).wait()


def _scatter_call(dest, seg, y2, n_rows):
    n, d = y2.shape
    t = TOKEN_TILE
    return pl.pallas_call(
        _scatter_kernel,
        grid_spec=pltpu.PrefetchScalarGridSpec(
            num_scalar_prefetch=2, grid=(n // t,),
            in_specs=[pl.BlockSpec((t, d), lambda i, dest, seg: (i, 0))],
            out_specs=pl.BlockSpec(memory_space=pl.ANY),
            scratch_shapes=[pltpu.VMEM((EXPERT_TILE, d), F32), pltpu.SemaphoreType.DMA, pltpu.SemaphoreType.DMA]),
        out_shape=jax.ShapeDtypeStruct((n_rows, d), F32), name="moe_scatter",
        compiler_params=pltpu.CompilerParams(dimension_semantics=("arbitrary",)),
    )(dest, seg, y2)


def _expert_kernel(block_e_ref, n_used_ref, x_ref, wg_ref, wu_ref, wd_ref, o_ref):
    del block_e_ref

    @pl.when(pl.program_id(0) < n_used_ref[0])
    def _():
        x = x_ref[...]
        gate = jnp.dot(x, wg_ref[...], preferred_element_type=F32)
        up = jnp.dot(x, wu_ref[...], preferred_element_type=F32)
        o_ref[...] = jnp.dot(jax.nn.silu(gate) * up, wd_ref[...], preferred_element_type=F32)

    @pl.when(pl.program_id(0) >= n_used_ref[0])
    def _():
        o_ref[...] = jnp.zeros_like(o_ref)


def _expert_call(layer, block_e, n_used, xb, wg, wu, wd):
    n_rows, d = xb.shape
    f = wg.shape[-1]
    tm = EXPERT_TILE
    rows_in = lambda b, be, nu: (jnp.minimum(b, nu[0] - 1), 0)
    wmap = lambda b, be, nu: (layer, be[b], 0, 0)
    wspec = lambda shape: pl.BlockSpec((None, None) + shape, wmap)
    return pl.pallas_call(
        _expert_kernel,
        grid_spec=pltpu.PrefetchScalarGridSpec(
            num_scalar_prefetch=2, grid=(n_rows // tm,),
            in_specs=[pl.BlockSpec((tm, d), rows_in), wspec((d, f)), wspec((d, f)), wspec((f, d))],
            out_specs=pl.BlockSpec((tm, d), lambda b, be, nu: (b, 0))),
        out_shape=jax.ShapeDtypeStruct((n_rows, d), F32), name="moe_experts",
        compiler_params=pltpu.CompilerParams(dimension_semantics=("arbitrary",),
                                             vmem_limit_bytes=EXPERT_VMEM_LIMIT),
    )(block_e, n_used, xb, wg, wu, wd)


def _combine_kernel(final, dest_ref, h_ref, gate_ref, g_ref, yb_ref, o_ref, buf_ref, sem):
    t = h_ref.shape[0]
    i = pl.program_id(0)
    n_tiles = pl.num_programs(0)
    n = t * n_tiles

    def issue_tile(tile, slot):
        for r in range(t):
            for k in range(TOP_K):
                _row_copy(yb_ref, dest_ref[k * n + tile * t + r], buf_ref.at[slot, k], r,
                          sem.at[slot]).start(priority=k)

    slot = i % 2

    @pl.when(i == 0)
    def _():
        issue_tile(0, 0)

    @pl.when(i + 1 < n_tiles)
    def _():
        issue_tile(i + 1, 1 - slot)

    for k in range(TOP_K):
        pltpu.make_async_copy(yb_ref.at[pl.ds(0, t), :], buf_ref.at[slot, k], sem.at[slot]).wait()
    gates = gate_ref[...]
    out = h_ref[...] + gates[:, 0:1] * buf_ref[slot, 0] + gates[:, 1:2] * buf_ref[slot, 1]
    if final:
        out = _rms(out, g_ref[...])
    o_ref[...] = out


def _combine_call(final, dest, h, gates, g, yb):
    n, d = h.shape
    t = TOKEN_TILE
    row = lambda i, dest: (i, 0)
    return pl.pallas_call(
        functools.partial(_combine_kernel, final),
        grid_spec=pltpu.PrefetchScalarGridSpec(
            num_scalar_prefetch=1, grid=(n // t,),
            in_specs=[pl.BlockSpec((t, d), row), pl.BlockSpec((t, TOP_K), row),
                      pl.BlockSpec((1, d), lambda i, dest: (0, 0)), pl.BlockSpec(memory_space=pl.ANY)],
            out_specs=pl.BlockSpec((t, d), row),
            scratch_shapes=[pltpu.VMEM((2, TOP_K, t, d), F32), pltpu.SemaphoreType.DMA((2,))]),
        out_shape=jax.ShapeDtypeStruct((n, d), F32), name="moe_combine",
        compiler_params=pltpu.CompilerParams(dimension_semantics=("arbitrary",)),
    )(dest, h, gates, g, yb)


def _plan(route, counts, n_rows):
    e = route[0:2].astype(jnp.int32)
    rank = route[2:4].astype(jnp.int32)
    gates = route[4:6].T
    cnt = counts[:, 0].astype(jnp.int32)
    pcnt = (cnt + EXPERT_TILE - 1) // EXPERT_TILE * EXPERT_TILE
    pends = jnp.cumsum(pcnt)
    pstarts = pends - pcnt
    eids = jnp.arange(N_EXPERTS, dtype=jnp.int32)
    start_of = jnp.sum(jnp.where(e[..., None] == eids, pstarts, 0), axis=-1)
    dest = (start_of + rank).reshape(-1)
    n_used = pends[-1] // EXPERT_TILE
    first_row = jnp.minimum(jnp.arange(n_rows // EXPERT_TILE, dtype=jnp.int32), n_used - 1) * EXPERT_TILE
    block_e = jnp.sum((pends[None, :] <= first_row[:, None]).astype(jnp.int32), axis=1)
    block_e = jnp.minimum(block_e, N_EXPERTS - 1)
    tail = n_rows - pends[-1:]
    seg = jnp.concatenate([pstarts, pends[-1:], cnt, 0 * tail, pcnt, tail]).astype(jnp.int32)
    return dest, seg, gates, block_e, n_used.astype(jnp.int32).reshape(1)


def kernel(x, norm_mix_g, norm_ffn_g, final_g, a_w_in, a_v_gain, a_w_s, a_b_s, a_w_out, b_w_in, b_w_grp, b_b_grp, b_scale, b_w_out, c_w_in, c_dw_w, c_dw_b, c_ln_g, c_ln_b, c_w_out, router_w, router_b, moe_w_gate, moe_w_up, moe_w_down):
    bsz, seq_len, d = x.shape
    depth = norm_mix_g.shape[0]
    n = bsz * seq_len
    n_rows = n * TOP_K + N_EXPERTS * EXPERT_TILE
    h = x.reshape(n, d)
    rwt = router_w.T.astype(BF16)
    rb = router_b.reshape(N_EXPERTS, 1)
    for i in range(depth):
        m, j = i % N_MIXERS, i // N_MIXERS
        if m == 0:
            kind = "gmlp"
            weights = (a_w_in[j].astype(BF16), a_v_gain[j].reshape(1, d), a_w_s[j],
                       a_b_s[j].reshape(A_HEADS, CHUNK, 1), a_w_out[j].astype(BF16))
        elif m == 1:
            kind = "pool"
            weights = (b_w_in[j].astype(BF16), b_w_grp[j].astype(BF16),
                       b_b_grp[j].reshape(B_GROUPS, 1, d // B_GROUPS), b_scale[j].reshape(1, d),
                       b_w_out[j].astype(BF16))
        else:
            kind = "conv"
            weights = (c_w_in[j].astype(BF16), c_dw_w[j], c_dw_b[j].reshape(1, d), c_ln_g[j].reshape(1, d),
                       c_ln_b[j].reshape(1, d), c_w_out[j].astype(BF16))
        hmid, y2, route, counts = _mixer_call(
            kind, h, norm_mix_g[i].reshape(1, d), norm_ffn_g[i].reshape(1, d), rwt, rb, weights, seq_len)
        dest, seg, gates, block_e, n_used = _plan(route, counts, n_rows)
        xb = _scatter_call(dest, seg, y2, n_rows)
        yb = _expert_call(i, block_e, n_used, xb, moe_w_gate, moe_w_up, moe_w_down)
        h = _combine_call(i == depth - 1, dest, hmid, gates, final_g.reshape(1, d), yb)
    return h.reshape(bsz, seq_len, d)
```
